```python
import math
import jax, jax.numpy as jnp
from jax import lax
import numpy as np

D_MODEL = 1024
BATCH = 2
SEQ = 8192
DEPTH = 4
DEC_BATCH = 32
DEC_SEQ = 1
PAST_LEN = 8192
PAGE_SIZE = 128

H_A = 8
HD_A = 64
H_IDX = 8
D_IDX = 64
TOPK_MAX = 256
Q_BLOCK = 128
ROPE_THETA = 10000.0
H_B = 4
DK_B = 128
DV_B = 128
H_C = 8
DK_C = 64
DV_C = 64
CONV_W = 4
CHUNK = 64
N_MEM = 256
H_X = 4
HD_X = 128
D_FF = 4 * D_MODEL
N_BRANCH = 3
EPS = 1e-6

W_A = H_A * HD_A
W_QI = H_IDX * D_IDX
W_BK = H_B * DK_B
W_BV = H_B * DV_B
W_CK = H_C * DK_C
W_CV = H_C * DV_C
CONV_DIM = 2 * W_CK + W_CV
W_X = H_X * HD_X
SPLITS = (W_A, W_A, W_A, W_QI, D_IDX, H_IDX,
          W_BK, W_BK, W_BV, W_BV,
          CONV_DIM, W_CV, H_C, H_C,
          N_BRANCH * D_MODEL)
N_IN = sum(SPLITS)

kernel_name = "hybrid_dsa_hgrn2_gdn_decoder_step"


def rmsnorm(x, g):
    xf = x.astype(jnp.float32)
    y = xf * lax.rsqrt(jnp.mean(xf * xf, axis=-1, keepdims=True) + EPS)
    return (y * g.astype(jnp.float32)).astype(x.dtype)


def l2norm(x):
    xf = x.astype(jnp.float32)
    return xf * lax.rsqrt(jnp.sum(xf * xf, axis=-1, keepdims=True) + EPS)


def split_cols(a):
    offs = [int(o) for o in np.cumsum(SPLITS)[:-1]]
    return jnp.split(a, offs, axis=-1)


def rope(x, pos):
    half = x.shape[-1] // 2
    inv = ROPE_THETA ** (-jnp.arange(half, dtype=jnp.float32) / half)
    ang = pos.astype(jnp.float32)[:, None] * inv[None, :]
    cos = jnp.cos(ang)[:, None, :]
    sin = jnp.sin(ang)[:, None, :]
    xf = x.astype(jnp.float32)
    x1, x2 = xf[..., :half], xf[..., half:]
    return jnp.concatenate([x1 * cos - x2 * sin, x2 * cos + x1 * sin], axis=-1).astype(x.dtype)


def gather_rows(rows, idx):
    return jax.vmap(lambda r, i: r[i])(rows, idx)


def index_scores(q_idx, w_idx, k_idx):
    s = jax.nn.relu(jnp.einsum('bthd,bsd->bths', q_idx, k_idx, preferred_element_type=jnp.float32))
    return jnp.einsum('bths,bth->bts', s, w_idx.astype(jnp.float32))


def sparse_attend(q, k_sel, v_sel, valid):
    logits = jnp.einsum('bthd,btkhd->bthk', q, k_sel, preferred_element_type=jnp.float32) * (HD_A ** -0.5)
    logits = jnp.where(valid[:, :, None, :], logits, -jnp.inf)
    p = jax.nn.softmax(logits, axis=-1).astype(v_sel.dtype)
    return jnp.einsum('bthk,btkhd->bthd', p, v_sel)


def dsa_prompt(q, k, v, q_idx, w_idx, k_idx):
    B, S = q.shape[0], q.shape[1]
    topk = min(TOPK_MAX, S // 4)
    nb = S // Q_BLOCK
    key_pos = jnp.arange(S)

    def blocks(a):
        return jnp.moveaxis(a.reshape((B, nb, Q_BLOCK) + a.shape[2:]), 1, 0)

    def one_block(args):
        qb, qib, wb, t0 = args
        pos_q = t0 + jnp.arange(Q_BLOCK)
        adm = key_pos[None, :] <= pos_q[:, None]
        sc = jnp.where(adm[None], index_scores(qib, wb, k_idx), -jnp.inf)
        _, idx = lax.top_k(sc, topk)
        valid = idx <= pos_q[None, :, None]
        return sparse_attend(qb, gather_rows(k, idx), gather_rows(v, idx), valid)

    out = lax.map(one_block, (blocks(q), blocks(q_idx), blocks(w_idx), jnp.arange(nb) * Q_BLOCK))
    return jnp.moveaxis(out, 0, 1).reshape(B, S, H_A, HD_A)


def dsa_sample(q, k_new, v_new, q_idx, w_idx, kidx_new, pool_k, pool_v, pool_kidx, page_table):
    B, T = q.shape[0], q.shape[1]
    past = page_table.shape[1] * PAGE_SIZE
    L = past + T
    topk = min(TOPK_MAX, L // 4)
    kidx_past = pool_kidx[page_table].reshape(B, past, D_IDX)
    sc = jnp.concatenate([index_scores(q_idx, w_idx, kidx_past),
                          index_scores(q_idx, w_idx, kidx_new)], axis=-1)
    pos_q = past + jnp.arange(T)
    adm = jnp.arange(L)[None, :] <= pos_q[:, None]
    sc = jnp.where(adm[None], sc, -jnp.inf)
    _, idx = lax.top_k(sc, topk)
    valid = idx <= pos_q[None, :, None]
    in_past = idx < past
    pidx = jnp.minimum(idx, past - 1)
    phys = jnp.take_along_axis(page_table, (pidx // PAGE_SIZE).reshape(B, -1), axis=1).reshape(idx.shape)
    off = pidx % PAGE_SIZE
    nidx = jnp.clip(idx - past, 0, T - 1)

    def select(pool, new):
        return jnp.where(in_past[..., None, None], pool[phys, off], gather_rows(new, nidx))

    return sparse_attend(q, select(pool_k, k_new), select(pool_v, v_new), valid)


def to_chunks(a, C):
    T = a.shape[1]
    nc = -(-T // C)
    a = jnp.pad(a, [(0, 0), (0, nc * C - T)] + [(0, 0)] * (a.ndim - 2))
    return jnp.moveaxis(a.reshape((a.shape[0], nc, C) + a.shape[2:]), 1, 0)


def from_chunks(o, T):
    o = jnp.moveaxis(o, 0, 1)
    return o.reshape((o.shape[0], -1) + o.shape[3:])[:, :T]


def hgrn2_chunked(q, k, v, logf, s0):
    T = q.shape[1]
    C = min(CHUNK, T)
    incl = jnp.tril(jnp.ones((C, C), dtype=bool))

    def step(S, xs):
        qc, kc, vc, gc = xs
        G = jnp.cumsum(gc, axis=1)
        diff = G[:, :, None] - G[:, None]
        decay = jnp.exp(jnp.where(incl[None, :, :, None, None], diff, -jnp.inf))
        A = jnp.einsum('bthd,bshd,btshd->bhts', qc, kc, decay)
        o = jnp.einsum('bhts,bshv->bthv', A, vc) + jnp.einsum('bthd,bhdv->bthv', qc * jnp.exp(G), S)
        g_last = G[:, -1]
        S = S * jnp.exp(g_last)[..., None] + jnp.einsum('bshd,bshv->bhdv', kc * jnp.exp(g_last[:, None] - G), vc)
        return S, o

    S, o = lax.scan(step, s0, (to_chunks(q, C), to_chunks(k, C), to_chunks(v, C), to_chunks(logf, C)))
    return from_chunks(o, T), S


def gdn_chunked(q, k, v, g, beta, s0):
    T = q.shape[1]
    C = min(CHUNK, T)
    incl = jnp.tril(jnp.ones((C, C), dtype=bool))
    strict = jnp.tril(jnp.ones((C, C), dtype=bool), -1)
    eye = jnp.eye(C, dtype=jnp.float32)

    def step(S, xs):
        qc, kc, vc, gc, bc = xs
        G = jnp.moveaxis(jnp.cumsum(gc, axis=1), 2, 1)
        diff = G[:, :, :, None] - G[:, :, None, :]
        decay = jnp.exp(jnp.where(incl, diff, -jnp.inf))
        kb = kc * bc[..., None]
        vb = vc * bc[..., None]
        A = jnp.where(strict, jnp.einsum('bthd,bshd->bhts', kb, kc) * decay, 0.0)
        Tinv = lax.linalg.triangular_solve(eye + A, jnp.broadcast_to(eye, A.shape),
                                           left_side=True, lower=True, unit_diagonal=True)
        expG = jnp.moveaxis(jnp.exp(G), 1, 2)[..., None]
        u = jnp.einsum('bhts,bshv->bhtv', Tinv, vb)
        w = jnp.einsum('bhts,bshd->bhtd', Tinv, kb * expG)
        v_new = u - jnp.einsum('bhtd,bhdv->bhtv', w, S)
        attn = jnp.einsum('bthd,bshd->bhts', qc, kc) * decay
        o = jnp.einsum('bthd,bhdv->bhtv', qc * expG, S) + jnp.einsum('bhts,bhsv->bhtv', attn, v_new)
        g_last = G[:, :, -1]
        k_dec = kc * jnp.moveaxis(jnp.exp(g_last[:, :, None] - G), 1, 2)[..., None]
        S = S * jnp.exp(g_last)[..., None, None] + jnp.einsum('bshd,bhsv->bhdv', k_dec, v_new)
        return S, jnp.moveaxis(o, 1, 2)

    S, o = lax.scan(step, s0, (to_chunks(q, C), to_chunks(k, C), to_chunks(v, C),
                               to_chunks(g, C), to_chunks(beta, C)))
    return from_chunks(o, T), S


def causal_conv(x, buf, w):
    T = x.shape[1]
    xx = jnp.concatenate([buf.astype(x.dtype), x], axis=1)
    y = xx[:, 0:T] * w[0]
    for i in range(1, CONV_W):
        y = y + xx[:, i:i + T] * w[i]
    return jax.nn.silu(y), xx[:, -(CONV_W - 1):]


def lower_bounds(lb_param):
    p = jax.nn.softmax(lb_param.astype(jnp.float32), axis=0)
    c = jnp.cumsum(p, axis=0)
    return c - c[0:1]


def token_mixers(h, pos, l, lb, s_hgrn, s_gdn, conv_buf, attend, w_in, g_kidx, g_hgrn,
                 conv_w, a_log, dt_bias, g_gdn, w_br_a, w_br_b, w_br_c, w_out):
    B, T, _ = h.shape
    f32 = jnp.float32
    (qa, ka, va, qi, ki, wi, qb, fb, ib, gb, qkv_c, zc, bc, ac, gates) = split_cols(h @ w_in[l])
    qa = rope(qa.reshape(B, T, H_A, HD_A), pos)
    ka = rope(ka.reshape(B, T, H_A, HD_A), pos)
    va = va.reshape(B, T, H_A, HD_A)
    qi = rope(qi.reshape(B, T, H_IDX, D_IDX), pos)
    ki = rope(rmsnorm(ki, g_kidx[l])[:, :, None, :], pos)[:, :, 0, :]
    wi = wi * (H_IDX ** -0.5 * D_IDX ** -0.5)
    o_a = attend(qa, ka, va, qi, wi, ki).reshape(B, T, W_A)
    qb = jax.nn.silu(qb.astype(f32)).reshape(B, T, H_B, DK_B)
    logf = jnp.logaddexp(jnp.log(lb[l]), jnp.log1p(-lb[l]) + jax.nn.log_sigmoid(fb.astype(f32)))
    logf = logf.reshape(B, T, H_B, DK_B)
    kb = -jnp.expm1(logf)
    vb = ib.astype(f32).reshape(B, T, H_B, DV_B)
    o_b, s_hgrn_new = hgrn2_chunked(qb, kb, vb, logf, s_hgrn.astype(f32))
    o_b = rmsnorm(o_b, g_hgrn[l].reshape(H_B, DV_B)) * jax.nn.silu(gb.astype(f32)).reshape(B, T, H_B, DV_B)
    o_b = o_b.reshape(B, T, W_BV).astype(h.dtype)
    qkv, conv_new = causal_conv(qkv_c, conv_buf, conv_w[l])
    qc, kc, vc = jnp.split(qkv, [W_CK, 2 * W_CK], axis=-1)
    qc = l2norm(qc.reshape(B, T, H_C, DK_C)) * (DK_C ** -0.5)
    kc = l2norm(kc.reshape(B, T, H_C, DK_C))
    vc = vc.astype(f32).reshape(B, T, H_C, DV_C)
    beta = jax.nn.sigmoid(bc.astype(f32))
    g = -jnp.exp(a_log[l].astype(f32)) * jax.nn.softplus(ac.astype(f32) + dt_bias[l].astype(f32))
    o_c, s_gdn_new = gdn_chunked(qc, kc, vc, g, beta, s_gdn.astype(f32))
    o_c = rmsnorm(o_c, g_gdn[l].reshape(H_C, DV_C)) * jax.nn.silu(zc.astype(f32)).reshape(B, T, H_C, DV_C)
    o_c = o_c.reshape(B, T, W_CV).astype(h.dtype)
    gt = jax.nn.sigmoid(gates.reshape(B, T, N_BRANCH, D_MODEL))
    m = gt[:, :, 0] * (o_a @ w_br_a[l]) + gt[:, :, 1] * (o_b @ w_br_b[l]) + gt[:, :, 2] * (o_c @ w_br_c[l])
    return m @ w_out[l], (ka, va, ki, s_hgrn_new, s_gdn_new, conv_new)


def cross_attend(h, mem_k, mem_v, w_q, w_o):
    B, T, _ = h.shape
    q = (h @ w_q).reshape(B, T, H_X, HD_X)
    logits = jnp.einsum('bthd,bmhd->bhtm', q, mem_k, preferred_element_type=jnp.float32) * (HD_X ** -0.5)
    p = jax.nn.softmax(logits, axis=-1).astype(mem_v.dtype)
    o = jnp.einsum('bhtm,bmhd->bthd', p, mem_v).reshape(B, T, W_X)
    return o @ w_o


def setup_inputs(seed: int = 0) -> dict:
    key = jax.random.key(seed)
    ks = iter(jax.random.split(key, 48))

    def nrm(shape, scale=1.0):
        return jax.random.normal(next(ks), shape, jnp.float32) * scale

    def gain(shape):
        return 1.0 + nrm(shape, 0.02)

    n_pages = PAST_LEN // PAGE_SIZE
    n_used = DEC_BATCH * n_pages
    n_pool = n_used + max(1, n_used // 4)
    page_table = jax.random.permutation(next(ks), n_pool)[:n_used].reshape(DEC_BATCH, n_pages).astype(jnp.int32)
    dt = jnp.exp(jax.random.uniform(next(ks), (DEPTH, H_C), jnp.float32, math.log(1e-3), math.log(1e-1)))
    dt_bias = dt + jnp.log(-jnp.expm1(-dt))
    a_log = jnp.log(jax.random.uniform(next(ks), (DEPTH, H_C), jnp.float32, 1.0, 16.0))
    return {
        'x_prompt': nrm((BATCH, SEQ, D_MODEL)),
        'x_sample': nrm((DEC_BATCH, DEC_SEQ, D_MODEL)),
        'cache_attn_k': nrm((DEPTH, n_pool, PAGE_SIZE, H_A, HD_A)),
        'cache_attn_v': nrm((DEPTH, n_pool, PAGE_SIZE, H_A, HD_A)),
        'cache_idx_k': nrm((DEPTH, n_pool, PAGE_SIZE, D_IDX)),
        'cache_mem_k': nrm((DEPTH, DEC_BATCH, N_MEM, H_X, HD_X)),
        'cache_mem_v': nrm((DEPTH, DEC_BATCH, N_MEM, H_X, HD_X)),
        'state_hgrn': nrm((DEPTH, DEC_BATCH, H_B, DK_B, DV_B), 0.5),
        'state_gdn': nrm((DEPTH, DEC_BATCH, H_C, DK_C, DV_C), 0.1),
        'state_conv': nrm((DEPTH, DEC_BATCH, CONV_W - 1, CONV_DIM)),
        'page_table': page_table,
        'mem_prompt': nrm((BATCH, N_MEM, D_MODEL)),
        'norm_mix': gain((DEPTH, D_MODEL)),
        'w_in': nrm((DEPTH, D_MODEL, N_IN), D_MODEL ** -0.5),
        'g_kidx': gain((DEPTH, D_IDX)),
        'lb_param': nrm((DEPTH, W_BK), 0.1),
        'g_hgrn': gain((DEPTH, W_BV)),
        'conv_w': nrm((DEPTH, CONV_W, CONV_DIM), CONV_W ** -0.5),
        'a_log': a_log,
        'dt_bias': dt_bias,
        'g_gdn': gain((DEPTH, W_CV)),
        'w_br_a': nrm((DEPTH, W_A, D_MODEL), W_A ** -0.5),
        'w_br_b': nrm((DEPTH, W_BV, D_MODEL), W_BV ** -0.5),
        'w_br_c': nrm((DEPTH, W_CV, D_MODEL), W_CV ** -0.5),
        'w_out': nrm((DEPTH, D_MODEL, D_MODEL), D_MODEL ** -0.5),
        'norm_cross': gain((DEPTH, D_MODEL)),
        'norm_mem': gain((DEPTH, D_MODEL)),
        'w_xq': nrm((DEPTH, D_MODEL, W_X), D_MODEL ** -0.5),
        'w_xk': nrm((DEPTH, D_MODEL, W_X), D_MODEL ** -0.5),
        'w_xv': nrm((DEPTH, D_MODEL, W_X), D_MODEL ** -0.5),
        'w_xo': nrm((DEPTH, W_X, D_MODEL), W_X ** -0.5),
        'norm_mlp': gain((DEPTH, D_MODEL)),
        'w_up': nrm((DEPTH, D_MODEL, D_FF), D_MODEL ** -0.5),
        'w_down': nrm((DEPTH, D_FF, D_MODEL), D_FF ** -0.5),
        'norm_final': gain((D_MODEL,)),
    }


def reference(x_prompt, x_sample, cache_attn_k, cache_attn_v, cache_idx_k, cache_mem_k, cache_mem_v,
              state_hgrn, state_gdn, state_conv, page_table, mem_prompt,
              norm_mix, w_in, g_kidx, lb_param, g_hgrn, conv_w, a_log, dt_bias, g_gdn,
              w_br_a, w_br_b, w_br_c, w_out, norm_cross, norm_mem, w_xq, w_xk, w_xv, w_xo,
              norm_mlp, w_up, w_down, norm_final):
    Bp, Tp = x_prompt.shape[0], x_prompt.shape[1]
    Bs, Ts = x_sample.shape[0], x_sample.shape[1]
    past = page_table.shape[1] * PAGE_SIZE
    pos_p = jnp.arange(Tp)
    pos_s = past + jnp.arange(Ts)
    lb = lower_bounds(lb_param)
    f32 = jnp.float32
    shared = (w_in, g_kidx, g_hgrn, conv_w, a_log, dt_bias, g_gdn, w_br_a, w_br_b, w_br_c, w_out)

    def channel_and_cross(x, l, mk, mv):
        x = x + cross_attend(rmsnorm(x, norm_cross[l]), mk, mv, w_xq[l], w_xo[l])
        h = rmsnorm(x, norm_mlp[l])
        return x + jnp.square(jax.nn.relu(h @ w_up[l])) @ w_down[l]

    xp, xs = x_prompt, x_sample
    kp, vp, ip, mkp, mvp, hp, gp, cp = [], [], [], [], [], [], [], []
    ks_, vs_, is_, hs_, gs_, cs_ = [], [], [], [], [], []
    for l in range(DEPTH):
        m, (ka, va, ki, sh, sg, cb) = token_mixers(
            rmsnorm(xp, norm_mix[l]), pos_p, l, lb,
            jnp.zeros((Bp, H_B, DK_B, DV_B), f32), jnp.zeros((Bp, H_C, DK_C, DV_C), f32),
            jnp.zeros((Bp, CONV_W - 1, CONV_DIM), xp.dtype), dsa_prompt, *shared)
        xp = xp + m
        mem_n = rmsnorm(mem_prompt, norm_mem[l])
        mk = (mem_n @ w_xk[l]).reshape(Bp, N_MEM, H_X, HD_X)
        mv = (mem_n @ w_xv[l]).reshape(Bp, N_MEM, H_X, HD_X)
        xp = channel_and_cross(xp, l, mk, mv)
        kp.append(ka); vp.append(va); ip.append(ki); mkp.append(mk); mvp.append(mv)
        hp.append(sh); gp.append(sg); cp.append(cb)

        def attend_s(q, k, v, qi, wi, ki, l=l):
            return dsa_sample(q, k, v, qi, wi, ki, cache_attn_k[l], cache_attn_v[l], cache_idx_k[l], page_table)

        m, (ka, va, ki, sh, sg, cb) = token_mixers(
            rmsnorm(xs, norm_mix[l]), pos_s, l, lb,
            state_hgrn[l], state_gdn[l], state_conv[l], attend_s, *shared)
        xs = xs + m
        xs = channel_and_cross(xs, l, cache_mem_k[l], cache_mem_v[l])
        ks_.append(ka); vs_.append(va); is_.append(ki)
        hs_.append(sh); gs_.append(sg); cs_.append(cb)

    y_prompt = rmsnorm(xp, norm_final)
    y_sample = rmsnorm(xs, norm_final)
    st = jnp.stack
    return (y_prompt, y_sample,
            st(kp), st(vp), st(ip), st(mkp), st(mvp), st(hp), st(gp), st(cp),
            st(ks_), st(vs_), st(is_), st(hs_), st(gs_), st(cs_))
```

```python
import functools
import math

import jax
import jax.numpy as jnp
import numpy as np
from jax import lax
from jax.experimental import pallas as pl
from jax.experimental.pallas import tpu as pltpu

F32 = jnp.float32
BF16 = jnp.bfloat16
I32 = jnp.int32

D_MODEL = 1024
DEPTH = 4
PAGE = 128
H_A, HD_A = 8, 64
H_IDX, D_IDX = 8, 64
TOPK = 256
ROPE_THETA = 10000.0
H_B, DK_B, DV_B = 4, 128, 128
H_C, DK_C, DV_C = 8, 64, 64
CONV_W = 4
N_MEM = 256
H_X, HD_X = 4, 128
D_FF = 4 * D_MODEL
EPS = 1e-6

W_A = H_A * HD_A
W_B = H_B * DK_B
W_C = H_C * DK_C
CONV_DIM = 3 * W_C
W_X = H_X * HD_X
SPLITS = (W_A, W_A, W_A, W_A, D_IDX, H_IDX, W_B, W_B, W_B, W_B, CONV_DIM, W_C, H_C, H_C, 3 * D_MODEL)
OFFS = tuple(int(o) for o in np.cumsum((0,) + SPLITS))
WG_A = 2176
WG_B = 2048
WG_C = 2176
WG_G = 3 * D_MODEL

LANES = 128
VMEM_LIMIT = 56 * 1024 * 1024
NEG_INF = float("-inf")
KEY_NEG_INF = -2139095041
HIGHEST = lax.Precision.HIGHEST

TQ = 128
SC = 256


def _cp(*sem):
    return pltpu.CompilerParams(dimension_semantics=sem, vmem_limit_bytes=VMEM_LIMIT)


def _rms_rows(x, g):
    return x * lax.rsqrt(jnp.mean(x * x, axis=-1, keepdims=True) + EPS) * g


def _silu(x):
    return x * jax.nn.sigmoid(x)


def _dot(a, b):
    return jnp.dot(a, b, preferred_element_type=F32)


def _dot_nt(a, b):
    return lax.dot_general(a, b, (((1,), (1,)), ((), ())), preferred_element_type=F32)


def _dot_exact(a, b):
    return jnp.dot(a, b, preferred_element_type=F32, precision=HIGHEST)


def _norm_linear_kernel(x_ref, g_ref, w_ref, o_ref):
    h = _rms_rows(x_ref[...], g_ref[...])
    o_ref[...] = _dot(h.astype(BF16), w_ref[...])


def norm_linear(x, g, w, tm):
    n, d = x.shape
    wd = w.shape[1]
    return pl.pallas_call(
        _norm_linear_kernel,
        grid=(n // tm,),
        in_specs=[pl.BlockSpec((tm, d), lambda i: (i, 0)),
                  pl.BlockSpec((1, d), lambda i: (0, 0)),
                  pl.BlockSpec((d, wd), lambda i: (0, 0))],
        out_specs=pl.BlockSpec((tm, wd), lambda i: (i, 0)),
        out_shape=jax.ShapeDtypeStruct((n, wd), F32),
        compiler_params=_cp("parallel"),
        name="norm_linear",
    )(x, g.reshape(1, d), w)


def _linear_residual_kernel(x_ref, a_ref, w_ref, o_ref):
    o_ref[...] = x_ref[...] + _dot(a_ref[...].astype(BF16), w_ref[...])


def linear_residual(x, a, w, tm):
    n, d = x.shape
    k = a.shape[1]
    return pl.pallas_call(
        _linear_residual_kernel,
        grid=(n // tm,),
        in_specs=[pl.BlockSpec((tm, d), lambda i: (i, 0)),
                  pl.BlockSpec((tm, k), lambda i: (i, 0)),
                  pl.BlockSpec((k, d), lambda i: (0, 0))],
        out_specs=pl.BlockSpec((tm, d), lambda i: (i, 0)),
        out_shape=jax.ShapeDtypeStruct((n, d), F32),
        compiler_params=_cp("parallel"),
        name="linear_residual",
    )(x, a, w)


def _mlp_kernel(x_ref, g_ref, wu_ref, wd_ref, o_ref, h_scr):
    j = pl.program_id(1)

    @pl.when(j == 0)
    def _():
        x = x_ref[...]
        h_scr[...] = _rms_rows(x, g_ref[...]).astype(BF16)
        o_ref[...] = x

    u = _dot(h_scr[...], wu_ref[...])
    a = jnp.square(jnp.maximum(u, 0.0)).astype(BF16)
    o_ref[...] += _dot(a, wd_ref[...])


def mlp(x, g, w_up, w_down, tm, tf):
    n, d = x.shape
    ff = w_up.shape[1]
    return pl.pallas_call(
        _mlp_kernel,
        grid=(n // tm, ff // tf),
        in_specs=[pl.BlockSpec((tm, d), lambda i, j: (i, 0)),
                  pl.BlockSpec((1, d), lambda i, j: (0, 0)),
                  pl.BlockSpec((d, tf), lambda i, j: (0, j)),
                  pl.BlockSpec((tf, d), lambda i, j: (j, 0))],
        out_specs=pl.BlockSpec((tm, d), lambda i, j: (i, 0)),
        out_shape=jax.ShapeDtypeStruct((n, d), F32),
        scratch_shapes=[pltpu.VMEM((tm, d), BF16)],
        compiler_params=_cp("parallel", "arbitrary"),
        name="mlp",
    )(x, g.reshape(1, d), w_up, w_down)


def _final_norm_kernel(x_ref, g_ref, o_ref):
    o_ref[...] = _rms_rows(x_ref[...], g_ref[...])


def final_norm(x, g, tm):
    n, d = x.shape
    return pl.pallas_call(
        _final_norm_kernel,
        grid=(n // tm,),
        in_specs=[pl.BlockSpec((tm, d), lambda i: (i, 0)), pl.BlockSpec((1, d), lambda i: (0, 0))],
        out_specs=pl.BlockSpec((tm, d), lambda i: (i, 0)),
        out_shape=jax.ShapeDtypeStruct((n, d), F32),
        compiler_params=_cp("parallel"),
        name="final_norm",
    )(x, g.reshape(1, d))


def _rope(x, c, s):
    w = x.shape[-1]
    lane = lax.broadcasted_iota(I32, x.shape, 1)
    first = (lane % HD_A) < (HD_A // 2)
    partner = jnp.where(first, pltpu.roll(x, w - HD_A // 2, 1), pltpu.roll(x, HD_A // 2, 1))
    return x * c + partner * s


def _dsa_prep_kernel(a_ref, c_ref, s_ref, gk_ref, q_ref, k_ref, kb_ref, vb_ref, qi_ref, tail_ref, ki2_ref):
    c = c_ref[...]
    s = s_ref[...]
    qa = a_ref[:, 0:W_A]
    ka = a_ref[:, W_A:2 * W_A]
    va = a_ref[:, 2 * W_A:3 * W_A]
    qi = a_ref[:, 3 * W_A:4 * W_A]
    t = a_ref[:, 4 * W_A:4 * W_A + LANES]
    q_ref[...] = (_rope(qa, c, s) * (HD_A ** -0.5)).astype(BF16)
    kr = _rope(ka, c, s)
    k_ref[...] = kr
    kb_ref[...] = kr.astype(BF16)
    vb_ref[...] = va.astype(BF16)
    qi_ref[...] = _rope(qi, c, s).astype(BF16)
    lane = lax.broadcasted_iota(I32, t.shape, 1)
    is_k = lane < D_IDX
    ms = jnp.sum(jnp.where(is_k, t * t, 0.0), axis=-1, keepdims=True) * (1.0 / D_IDX)
    kin = t * lax.rsqrt(ms + EPS) * gk_ref[...]
    kir = _rope(kin, c[:, :LANES], s[:, :LANES])
    is_w = jnp.logical_and(lane >= D_IDX, lane < D_IDX + H_IDX)
    tail_ref[...] = kir + jnp.where(is_w, t * (H_IDX ** -0.5 * D_IDX ** -0.5), 0.0)
    ki2_ref[...] = (kir + pltpu.roll(kir, D_IDX, 1)).astype(BF16)


def dsa_prep(a_raw, cos, sin, gk, tm, n_tab_tiles):
    n = a_raw.shape[0]
    row = lambda w: pl.BlockSpec((tm, w), lambda i: (i, 0))
    tab = pl.BlockSpec((tm, W_A), lambda i: (i % n_tab_tiles, 0))
    sds = lambda w, dt: jax.ShapeDtypeStruct((n, w), dt)
    return pl.pallas_call(
        _dsa_prep_kernel,
        grid=(n // tm,),
        in_specs=[row(WG_A), tab, tab, pl.BlockSpec((1, LANES), lambda i: (0, 0))],
        out_specs=[row(W_A), row(W_A), row(W_A), row(W_A), row(W_A), row(LANES), row(LANES)],
        out_shape=[sds(W_A, BF16), sds(W_A, F32), sds(W_A, BF16), sds(W_A, BF16), sds(W_A, BF16),
                   sds(LANES, F32), sds(LANES, BF16)],
        compiler_params=_cp("parallel"),
        name="dsa_prep",
    )(a_raw, cos, sin, gk)


def _sort_key(x):
    i = pltpu.bitcast(x, I32)
    return i ^ (lax.shift_right_arithmetic(i, 31) & 0x7FFFFFFF)


def _kth_largest_key(count_ge, shape):
    def body(b, t):
        cand = t + lax.shift_left(jnp.int32(1), 31 - b)
        return jnp.where(count_ge(cand) >= TOPK, cand, t)
    return lax.fori_loop(0, 32, body, jnp.full(shape, -2 ** 31, I32))


def _dsa_prompt_kernel(q_ref, qi_ref, tail_ref, ki2_ref, k_ref, v_ref, o_ref,
                       qis_scr, qm_scr, wb_scr, key_scr, m_scr, l_scr, acc_scr):
    i = pl.program_id(1)
    t0 = i * TQ
    nchunk = (t0 + TQ + SC - 1) // SC
    lane = lax.broadcasted_iota(I32, (TQ, LANES), 1)
    half = lane // HD_A

    for h in range(H_A):
        p, hl = h // 2, h % 2
        sl = slice(p * LANES, (p + 1) * LANES)
        qis_scr[h * TQ:(h + 1) * TQ, :] = jnp.where(half == hl, qi_ref[:, sl].astype(F32), 0.0).astype(BF16)
        qm_scr[h * TQ:(h + 1) * TQ, :] = jnp.where(half == hl, q_ref[:, sl].astype(F32), 0.0).astype(BF16)
        wb_scr[h] = jnp.broadcast_to(tail_ref[:, D_IDX + h:D_IDX + h + 1], (TQ, SC))

    row_t = t0 + lax.broadcasted_iota(I32, (TQ, SC), 0)
    col_s = lax.broadcasted_iota(I32, (TQ, SC), 1)

    def score_chunk(c, carry):
        base = pl.multiple_of(c * SC, SC)
        s_all = _dot_nt(qis_scr[...], ki2_ref[pl.ds(base, SC), :])
        acc = jnp.zeros((TQ, SC), F32)
        for h in range(H_IDX):
            acc = acc + wb_scr[h] * jnp.maximum(s_all[h * TQ:(h + 1) * TQ, :], 0.0)
        sc = jnp.where(col_s + base <= row_t, acc, NEG_INF)
        key_scr[c] = _sort_key(sc)
        return carry

    lax.fori_loop(0, nchunk, score_chunk, 0)

    def count(pred):
        def body(c, part):
            return part + jnp.where(pred(key_scr[c], c), 1, 0)
        part = lax.fori_loop(0, nchunk, body, jnp.zeros((TQ, SC), I32))
        return jnp.sum(part, axis=1, keepdims=True)

    def count_ge(cand):
        cb = jnp.broadcast_to(cand, (TQ, SC))
        return count(lambda kx, c: kx >= cb)

    thr = _kth_largest_key(count_ge, (TQ, 1))
    thr_b = jnp.broadcast_to(thr, (TQ, SC))
    n_ge = count(lambda kx, c: kx >= thr_b)
    excess = jnp.logical_and(n_ge > TOPK, thr > KEY_NEG_INF)

    @pl.when(jnp.max(excess.astype(I32)) > 0)
    def _():
        need = TOPK - count(lambda kx, c: kx > thr_b)

        def body(b, y):
            cand = y + lax.shift_left(jnp.int32(1), 13 - b)
            cb = jnp.broadcast_to(cand, (TQ, SC))
            f = count(lambda kx, c: jnp.logical_and(kx == thr_b, col_s + c * SC < cb))
            return jnp.where(f < need, cand, y)

        cut = lax.fori_loop(0, 14, body, jnp.zeros((TQ, 1), I32)) + 1
        cut = jnp.where(excess, cut, jnp.int32(2 ** 30))
        cut_b = jnp.broadcast_to(cut, (TQ, SC))

        def drop(c, carry):
            kx = key_scr[c]
            dropped = jnp.logical_and(kx == thr_b, col_s + c * SC >= cut_b)
            key_scr[c] = jnp.where(dropped, KEY_NEG_INF, kx)
            return carry

        lax.fori_loop(0, nchunk, drop, 0)

    m_scr[...] = jnp.full(m_scr.shape, NEG_INF, F32)
    l_scr[...] = jnp.zeros(l_scr.shape, F32)
    acc_scr[...] = jnp.zeros(acc_scr.shape, F32)

    def attend_chunk(c, carry):
        base = pl.multiple_of(c * SC, SC)
        sel = jnp.logical_and(key_scr[c] >= thr_b, col_s + base <= row_t)
        for p in range(H_A // 2):
            sl = slice(p * LANES, (p + 1) * LANES)
            kc = k_ref[pl.ds(base, SC), sl]
            vc = v_ref[pl.ds(base, SC), sl]
            lg = _dot_nt(qm_scr[2 * p * TQ:(2 * p + 2) * TQ, :], kc)
            for hl in range(2):
                h = 2 * p + hl
                x = jnp.where(sel, lg[hl * TQ:(hl + 1) * TQ, :], NEG_INF)
                m_old = m_scr[h]
                m_new = jnp.maximum(m_old, jnp.max(x, axis=1, keepdims=True))
                m_safe = jnp.where(m_new == NEG_INF, 0.0, m_new)
                pr = jnp.exp(x - m_safe)
                alpha = jnp.exp(m_old - m_safe)
                l_scr[h] = alpha * l_scr[h] + jnp.sum(pr, axis=1, keepdims=True)
                acc_scr[h] = alpha * acc_scr[h] + _dot(pr.astype(BF16), vc)
                m_scr[h] = m_new
        return carry

    lax.fori_loop(0, nchunk, attend_chunk, 0)

    for p in range(H_A // 2):
        o0 = acc_scr[2 * p] / l_scr[2 * p]
        o1 = acc_scr[2 * p + 1] / l_scr[2 * p + 1]
        o_ref[:, p * LANES:(p + 1) * LANES] = jnp.where(half == 0, o0, o1)


def dsa_prompt(q, qi, tail, ki2, kb, vb):
    b, t, _ = q.shape
    blk = lambda w: pl.BlockSpec((None, TQ, w), lambda bi, i: (bi, i, 0))
    full = lambda w: pl.BlockSpec((None, t, w), lambda bi, i: (bi, 0, 0), pipeline_mode=pl.Buffered(1))
    return pl.pallas_call(
        _dsa_prompt_kernel,
        grid=(b, t // TQ),
        in_specs=[blk(W_A), blk(W_A), blk(LANES), full(LANES), full(W_A), full(W_A)],
        out_specs=blk(W_A),
        out_shape=jax.ShapeDtypeStruct((b, t, W_A), F32),
        scratch_shapes=[pltpu.VMEM((H_IDX * TQ, LANES), BF16),
                        pltpu.VMEM((H_A * TQ, LANES), BF16),
                        pltpu.VMEM((H_IDX, TQ, SC), F32),
                        pltpu.VMEM((t // SC, TQ, SC), I32),
                        pltpu.VMEM((H_A, TQ, 1), F32),
                        pltpu.VMEM((H_A, TQ, 1), F32),
                        pltpu.VMEM((H_A, TQ, LANES), F32)],
        compiler_params=_cp("parallel", "arbitrary"),
        name="dsa_prompt",
    )(q, qi, tail, ki2, kb, vb)


PG = 8


def _head_rows(row, n_heads, width):
    full = jnp.broadcast_to(row, (n_heads, n_heads * width))
    hh = lax.broadcasted_iota(I32, full.shape, 0)
    ll = lax.broadcasted_iota(I32, full.shape, 1) // width
    return jnp.where(hh == ll, full, 0.0)


def _dsa_sample_score_kernel(pt_ref, qi_ref, w_ref, *refs):
    page_refs, o_ref = refs[:PG], refs[PG]
    j = pl.program_id(1)
    qh = qi_ref[0]
    w_col = w_ref[0]
    for g in range(PG):
        kp = page_refs[g][0].astype(BF16)
        s = jnp.maximum(_dot_nt(qh, kp), 0.0)
        o_ref[0, :, pl.ds(pl.multiple_of((j * PG + g) * PAGE, PAGE), PAGE)] = jnp.sum(
            w_col * s, axis=0, keepdims=True)


def dsa_sample_scores(page_table, qi, tail, pool_kidx):
    b, n_pages = page_table.shape
    page_spec = lambda g: pl.BlockSpec((1, PAGE, D_IDX), lambda bi, j, pt: (pt[bi, j * PG + g], 0, 0))
    return pl.pallas_call(
        _dsa_sample_score_kernel,
        grid_spec=pltpu.PrefetchScalarGridSpec(
            num_scalar_prefetch=1,
            grid=(b, n_pages // PG),
            in_specs=[pl.BlockSpec((1, H_IDX, D_IDX), lambda bi, j, pt: (bi, 0, 0)),
                      pl.BlockSpec((1, H_IDX, 1), lambda bi, j, pt: (bi, 0, 0))]
                     + [page_spec(g) for g in range(PG)],
            out_specs=pl.BlockSpec((1, 1, n_pages * PAGE), lambda bi, j, pt: (bi, 0, 0)),
        ),
        out_shape=jax.ShapeDtypeStruct((b, 1, n_pages * PAGE), F32),
        compiler_params=_cp("parallel", "arbitrary"),
        name="dsa_sample_scores",
    )(page_table, qi.reshape(b, H_IDX, D_IDX), tail[:, D_IDX:D_IDX + H_IDX].reshape(b, H_IDX, 1),
      *([pool_kidx] * PG))


def _dsa_sample_select_kernel(sc_ref, qi_ref, tail_ref, bias_ref, newsel_ref):
    sc = sc_ref[...]
    nb, past = sc.shape
    qi = qi_ref[...]
    ki = tail_ref[...]
    lane = lax.broadcasted_iota(I32, ki.shape, 1)
    kidx = jnp.where(lane < D_IDX, ki, 0.0)
    kidx = (kidx + pltpu.roll(kidx, D_IDX, 1)).astype(BF16).astype(F32)
    sc_new = jnp.zeros((nb, 1), F32)
    for h in range(H_IDX):
        p, hl = h // 2, h % 2
        qh = qi[:, p * LANES:(p + 1) * LANES].astype(F32)
        d = jnp.sum(jnp.where(lane // D_IDX == hl, qh * kidx, 0.0), axis=1, keepdims=True)
        w = jnp.sum(jnp.where(lane == D_IDX + h, ki, 0.0), axis=1, keepdims=True)
        sc_new = sc_new + w * jnp.maximum(d, 0.0)
    key = _sort_key(sc + 0.0)
    key_new = _sort_key(sc_new + 0.0)

    def count_ge(cand):
        return (jnp.sum(jnp.where(key >= cand, 1, 0), axis=1, keepdims=True)
                + jnp.where(key_new >= cand, 1, 0))

    thr = _kth_largest_key(count_ge, (nb, 1))
    n_gt = jnp.sum(jnp.where(key > thr, 1, 0), axis=1, keepdims=True) + jnp.where(key_new > thr, 1, 0)
    need = TOPK - n_gt
    eq = key == thr
    pos = lax.broadcasted_iota(I32, sc.shape, 1)

    def body(b, y):
        cand = y + lax.shift_left(jnp.int32(1), 13 - b)
        f = jnp.sum(jnp.where(jnp.logical_and(eq, pos < cand), 1, 0), axis=1, keepdims=True)
        return jnp.where(f < need, cand, y)

    cut = lax.fori_loop(0, 14, body, jnp.zeros((nb, 1), I32)) + 1
    sel = jnp.logical_or(key > thr, jnp.logical_and(eq, pos < cut))
    n_sel = jnp.sum(jnp.where(sel, 1, 0), axis=1, keepdims=True)
    bias_ref[...] = jnp.where(sel, 0.0, NEG_INF)
    new_sel = jnp.logical_or(key_new > thr, jnp.logical_and(key_new == thr, n_sel < TOPK))
    newsel_ref[...] = jnp.broadcast_to(jnp.where(new_sel, 0.0, NEG_INF), newsel_ref.shape)


def dsa_sample_select(sc, qi, tail):
    b, past = sc.shape
    return pl.pallas_call(
        _dsa_sample_select_kernel,
        out_shape=[jax.ShapeDtypeStruct((b, past), F32), jax.ShapeDtypeStruct((b, LANES), F32)],
        compiler_params=pltpu.CompilerParams(vmem_limit_bytes=VMEM_LIMIT),
        name="dsa_sample_select",
    )(sc, qi, tail)


def _dsa_sample_attn_kernel(pt_ref, q_ref, kn_ref, vn_ref, bias_ref, nsel_ref, *refs):
    k_refs, v_refs, o_ref = refs[:PG], refs[PG:2 * PG], refs[2 * PG]
    m_scr, l_scr, acc_scr = refs[2 * PG + 1:]
    j = pl.program_id(1)
    qf = _head_rows(q_ref[0].astype(F32), H_A, HD_A)
    qblk = qf.astype(BF16)

    @pl.when(j == 0)
    def _():
        lg = jnp.sum(qf * kn_ref[0].astype(BF16).astype(F32), axis=1, keepdims=True)
        lg = lg + nsel_ref[0][:, 0:1]
        m_scr[...] = lg
        pr = jnp.where(lg == NEG_INF, 0.0, 1.0)
        l_scr[...] = pr
        acc_scr[...] = pr * vn_ref[0].astype(BF16).astype(F32)

    for g in range(PG):
        kp = k_refs[g][0].astype(BF16)
        vp = v_refs[g][0].astype(BF16)
        x = _dot_nt(qblk, kp) + bias_ref[0, :, pl.ds(pl.multiple_of((j * PG + g) * PAGE, PAGE), PAGE)]
        m_old = m_scr[...]
        m_new = jnp.maximum(m_old, jnp.max(x, axis=1, keepdims=True))
        m_safe = jnp.where(m_new == NEG_INF, 0.0, m_new)
        pr = jnp.exp(x - m_safe)
        alpha = jnp.exp(m_old - m_safe)
        l_scr[...] = alpha * l_scr[...] + jnp.sum(pr, axis=1, keepdims=True)
        acc_scr[...] = alpha * acc_scr[...] + _dot(pr.astype(BF16), vp)
        m_scr[...] = m_new

    @pl.when(j == pl.num_programs(1) - 1)
    def _():
        o = acc_scr[...] / l_scr[...]
        hh = lax.broadcasted_iota(I32, o.shape, 0)
        ll = lax.broadcasted_iota(I32, o.shape, 1) // HD_A
        o_ref[0] = jnp.sum(jnp.where(hh == ll, o, 0.0), axis=0, keepdims=True)


def dsa_sample_attend(page_table, q, k_new, v_new, bias, new_sel, pool_k, pool_v):
    b, n_pages = page_table.shape
    page_spec = lambda g: pl.BlockSpec((1, PAGE, W_A), lambda bi, j, pt: (pt[bi, j * PG + g], 0, 0))
    row = lambda w: pl.BlockSpec((1, 1, w), lambda bi, j, pt: (bi, 0, 0))
    return pl.pallas_call(
        _dsa_sample_attn_kernel,
        grid_spec=pltpu.PrefetchScalarGridSpec(
            num_scalar_prefetch=1,
            grid=(b, n_pages // PG),
            in_specs=[row(W_A), row(W_A), row(W_A), row(n_pages * PAGE), row(LANES)]
                     + [page_spec(g) for g in range(PG)] * 2,
            out_specs=row(W_A),
            scratch_shapes=[pltpu.VMEM((H_A, 1), F32), pltpu.VMEM((H_A, 1), F32),
                            pltpu.VMEM((H_A, W_A), F32)],
        ),
        out_shape=jax.ShapeDtypeStruct((b, 1, W_A), F32),
        compiler_params=_cp("parallel", "arbitrary"),
        name="dsa_sample_attend",
    )(page_table, q.reshape(b, 1, W_A), k_new.reshape(b, 1, W_A), v_new.reshape(b, 1, W_A),
      bias.reshape(b, 1, n_pages * PAGE), new_sel.reshape(b, 1, LANES),
      *([pool_k] * PG), *([pool_v] * PG))


TB = 256


def _hgrn_prompt_kernel(b_ref, lb_ref, o_ref, s_ref, z_scr, q_scr, f_scr, k_scr):
    i = pl.program_id(1)

    @pl.when(i == 0)
    def _():
        z_scr[...] = jnp.zeros(z_scr.shape, F32)

    lb = lb_ref[...]
    fb = b_ref[:, W_B:2 * W_B]
    q_scr[...] = _silu(b_ref[:, 0:W_B])
    f_scr[...] = lb + (1.0 - lb) * jax.nn.sigmoid(fb)
    k_scr[...] = (1.0 - lb) * jax.nn.sigmoid(-fb)

    def step(t, carry):
        qrow = q_scr[pl.ds(t, 1), :]
        frow = f_scr[pl.ds(t, 1), :]
        krow = k_scr[pl.ds(t, 1), :]
        vrow = b_ref[pl.ds(t, 1), 2 * W_B:3 * W_B]
        outs = []
        for h in range(H_B):
            sl = slice(h * DK_B, (h + 1) * DK_B)
            rot = lambda r: pltpu.roll(jnp.broadcast_to(r[:, sl], (DK_B, DV_B)), 0, 1, stride=1, stride_axis=0)
            z = rot(frow) * z_scr[h] + rot(krow) * vrow[:, sl]
            z_scr[h] = z
            outs.append(jnp.sum(rot(qrow) * z, axis=0, keepdims=True))
        o_ref[pl.ds(t, 1), :] = jnp.concatenate(outs, axis=1)
        return carry

    lax.fori_loop(0, b_ref.shape[0], step, 0)

    @pl.when(i == pl.num_programs(1) - 1)
    def _():
        s_ref[...] = z_scr[...]


def _unrotate(z):
    n = z.shape[-1]
    d = jnp.arange(n)[:, None]
    v = jnp.arange(n)[None, :]
    idx = jnp.broadcast_to((v - d) % n, z.shape)
    return jnp.take_along_axis(z, idx, axis=-2)


def hgrn_prompt(b_raw, lb):
    b, t, _ = b_raw.shape
    o, z = pl.pallas_call(
        _hgrn_prompt_kernel,
        grid=(b, t // TB),
        in_specs=[pl.BlockSpec((None, TB, WG_B), lambda bi, i: (bi, i, 0)),
                  pl.BlockSpec((1, W_B), lambda bi, i: (0, 0))],
        out_specs=[pl.BlockSpec((None, TB, W_B), lambda bi, i: (bi, i, 0)),
                   pl.BlockSpec((None, H_B, DK_B, DV_B), lambda bi, i: (bi, 0, 0, 0))],
        out_shape=[jax.ShapeDtypeStruct((b, t, W_B), F32),
                   jax.ShapeDtypeStruct((b, H_B, DK_B, DV_B), F32)],
        scratch_shapes=[pltpu.VMEM((H_B, DK_B, DV_B), F32),
                        pltpu.VMEM((TB, W_B), F32), pltpu.VMEM((TB, W_B), F32), pltpu.VMEM((TB, W_B), F32)],
        compiler_params=_cp("parallel", "arbitrary"),
        name="hgrn_prompt",
    )(b_raw, lb.reshape(1, W_B))
    return o, _unrotate(z)


def _hgrn_sample_kernel(qb_ref, fb_ref, ib_ref, lb_ref, s_ref, o_ref, sn_ref):
    lb = lb_ref[...]
    fb = fb_ref[0]
    f = lb + (1.0 - lb) * jax.nn.sigmoid(fb)
    kk = (1.0 - lb) * jax.nn.sigmoid(-fb)
    s_new = f * s_ref[0] + kk * ib_ref[0]
    sn_ref[0] = s_new
    o_ref[0] = jnp.sum(_silu(qb_ref[0]) * s_new, axis=1, keepdims=True)


def hgrn_sample(b_raw, lb, state):
    b = b_raw.shape[0]
    col = lambda a: a.reshape(b, H_B, DK_B, 1)
    cspec = pl.BlockSpec((1, H_B, DK_B, 1), lambda i: (i, 0, 0, 0))
    rspec = pl.BlockSpec((1, H_B, 1, DV_B), lambda i: (i, 0, 0, 0))
    sspec = pl.BlockSpec((1, H_B, DK_B, DV_B), lambda i: (i, 0, 0, 0))
    o, s_new = pl.pallas_call(
        _hgrn_sample_kernel,
        grid=(b,),
        in_specs=[cspec, cspec, rspec, pl.BlockSpec((H_B, DK_B, 1), lambda i: (0, 0, 0)), sspec],
        out_specs=[rspec, sspec],
        out_shape=[jax.ShapeDtypeStruct((b, H_B, 1, DV_B), F32),
                   jax.ShapeDtypeStruct((b, H_B, DK_B, DV_B), F32)],
        compiler_params=_cp("parallel"),
        name="hgrn_sample",
    )(col(b_raw[:, 0:W_B]), col(b_raw[:, W_B:2 * W_B]), b_raw[:, 2 * W_B:3 * W_B].reshape(b, H_B, 1, DV_B),
      lb.reshape(H_B, DK_B, 1), state)
    return o.reshape(b, W_B), s_new


def _head_sum_matrix():
    r = lax.broadcasted_iota(I32, (W_C, W_C), 0) // DK_C
    c = lax.broadcasted_iota(I32, (W_C, W_C), 1) // DK_C
    return jnp.where(r == c, 1.0, 0.0).astype(F32)


def _gdn_prep_kernel(c_ref, halo_ref, buf_ref, w_ref, al_ref, dt_ref, q_ref, k_ref, v_ref, eg_ref, be_ref):
    i = pl.program_id(1)
    tm = c_ref.shape[0]
    prev = jnp.where(i == 0, buf_ref[...], halo_ref[:, 0:CONV_DIM])
    win = jnp.concatenate([prev, c_ref[:, 0:CONV_DIM]], axis=0)
    y = jnp.zeros((tm, CONV_DIM), F32)
    for tap in range(CONV_W):
        off = 8 - (CONV_W - 1) + tap
        y = y + win[off:off + tm, :] * w_ref[tap:tap + 1, :]
    y = _silu(y)
    ones = _head_sum_matrix()
    qc = y[:, 0:W_C]
    kc = y[:, W_C:2 * W_C]
    q_ref[...] = qc * lax.rsqrt(_dot_exact(qc * qc, ones) + EPS) * (DK_C ** -0.5)
    k_ref[...] = kc * lax.rsqrt(_dot_exact(kc * kc, ones) + EPS)
    v_ref[...] = y[:, 2 * W_C:3 * W_C]
    tail = c_ref[:, CONV_DIM + W_C:CONV_DIM + W_C + LANES]
    r = lax.broadcasted_iota(I32, (LANES, W_C), 0)
    cc = lax.broadcasted_iota(I32, (LANES, W_C), 1) // DK_C
    bcl = _dot_exact(tail, jnp.where(r == cc, 1.0, 0.0).astype(F32))
    acl = _dot_exact(tail, jnp.where(r == cc + H_C, 1.0, 0.0).astype(F32))
    be_ref[...] = jax.nn.sigmoid(bcl)
    x = acl + dt_ref[...]
    softplus = jnp.maximum(x, 0.0) + jnp.log1p(jnp.exp(-jnp.abs(x)))
    eg_ref[...] = jnp.exp(-jnp.exp(al_ref[...]) * softplus)


def gdn_prep(c_raw, buf8, conv_w, a_log_l, dt_bias_l, tm):
    b, t, _ = c_raw.shape
    blk = lambda w: pl.BlockSpec((None, tm, w), lambda bi, i: (bi, i, 0))
    sds = jax.ShapeDtypeStruct((b, t, W_C), F32)
    return pl.pallas_call(
        _gdn_prep_kernel,
        grid=(b, t // tm),
        in_specs=[blk(WG_C),
                  pl.BlockSpec((None, 8, WG_C), lambda bi, i: (bi, jnp.maximum(i * (tm // 8) - 1, 0), 0)),
                  pl.BlockSpec((None, 8, CONV_DIM), lambda bi, i: (bi, 0, 0)),
                  pl.BlockSpec((CONV_W, CONV_DIM), lambda bi, i: (0, 0)),
                  pl.BlockSpec((1, W_C), lambda bi, i: (0, 0)),
                  pl.BlockSpec((1, W_C), lambda bi, i: (0, 0))],
        out_specs=[blk(W_C)] * 5,
        out_shape=[sds] * 5,
        compiler_params=_cp("parallel", "arbitrary"),
        name="gdn_prep",
    )(c_raw, c_raw, buf8, conv_w, a_log_l, dt_bias_l)


def _gdn_prompt_kernel(q_ref, k_ref, v_ref, eg_ref, be_ref, o_ref, s_ref, z_scr):
    i = pl.program_id(1)

    @pl.when(i == 0)
    def _():
        z_scr[...] = jnp.zeros(z_scr.shape, F32)

    ri = lax.broadcasted_iota(I32, (DK_C, LANES), 0)
    lj = lax.broadcasted_iota(I32, (DK_C, LANES), 1) % DV_C
    keep = lj >= ri

    def rot(r):
        a = pltpu.roll(jnp.broadcast_to(r, (DK_C, LANES)), 0, 1, stride=1, stride_axis=0)
        return jnp.where(keep, a, pltpu.roll(a, DV_C, 1))

    def step(t, carry):
        qrow = q_ref[pl.ds(t, 1), :]
        krow = k_ref[pl.ds(t, 1), :]
        vrow = v_ref[pl.ds(t, 1), :]
        egrow = eg_ref[pl.ds(t, 1), :]
        berow = be_ref[pl.ds(t, 1), :]
        outs = []
        for p in range(H_C // 2):
            sl = slice(p * LANES, (p + 1) * LANES)
            kk = rot(krow[:, sl])
            z = z_scr[p]
            ks = jnp.sum(kk * z, axis=0, keepdims=True)
            u = berow[:, sl] * (vrow[:, sl] - egrow[:, sl] * ks)
            z = egrow[:, sl] * z + kk * u
            z_scr[p] = z
            outs.append(jnp.sum(rot(qrow[:, sl]) * z, axis=0, keepdims=True))
        o_ref[pl.ds(t, 1), :] = jnp.concatenate(outs, axis=1)
        return carry

    lax.fori_loop(0, q_ref.shape[0], step, 0)

    @pl.when(i == pl.num_programs(1) - 1)
    def _():
        s_ref[...] = z_scr[...]


def gdn_prompt(q, k, v, eg, be):
    b, t, _ = q.shape
    blk = pl.BlockSpec((None, TB, W_C), lambda bi, i: (bi, i, 0))
    o, z = pl.pallas_call(
        _gdn_prompt_kernel,
        grid=(b, t // TB),
        in_specs=[blk] * 5,
        out_specs=[blk, pl.BlockSpec((None, H_C // 2, DK_C, LANES), lambda bi, i: (bi, 0, 0, 0))],
        out_shape=[jax.ShapeDtypeStruct((b, t, W_C), F32),
                   jax.ShapeDtypeStruct((b, H_C // 2, DK_C, LANES), F32)],
        scratch_shapes=[pltpu.VMEM((H_C // 2, DK_C, LANES), F32)],
        compiler_params=_cp("parallel", "arbitrary"),
        name="gdn_prompt",
    )(q, k, v, eg, be)
    z = z.reshape(b, H_C // 2, DK_C, 2, DV_C).transpose(0, 1, 3, 2, 4).reshape(b, H_C, DK_C, DV_C)
    return o, _unrotate(z)


def _gdn_sample_kernel(qk_ref, vv_ref, wqk_ref, wv_ref, bc_ref, ac_ref, al_ref, dt_ref, s_ref, o_ref, sn_ref):
    yqk = jnp.zeros(qk_ref.shape[2:], F32)
    yv = jnp.zeros(vv_ref.shape[2:], F32)
    for tap in range(CONV_W):
        yqk = yqk + qk_ref[0, tap] * wqk_ref[tap]
        yv = yv + vv_ref[0, tap] * wv_ref[tap]
    yqk = _silu(yqk)
    vrow = _silu(yv)
    yq, yk = yqk[0:H_C], yqk[H_C:2 * H_C]
    qcol = yq * lax.rsqrt(jnp.sum(yq * yq, axis=1, keepdims=True) + EPS) * (DK_C ** -0.5)
    kcol = yk * lax.rsqrt(jnp.sum(yk * yk, axis=1, keepdims=True) + EPS)
    beta = jax.nn.sigmoid(bc_ref[0])
    x = ac_ref[0] + dt_ref[...]
    eg = jnp.exp(-jnp.exp(al_ref[...]) * (jnp.maximum(x, 0.0) + jnp.log1p(jnp.exp(-jnp.abs(x)))))
    s = s_ref[0]
    ks = jnp.sum(kcol * s, axis=1, keepdims=True)
    u = beta * (vrow - eg * ks)
    s_new = eg * s + kcol * u
    sn_ref[0] = s_new
    o_ref[0] = jnp.sum(qcol * s_new, axis=1, keepdims=True)


def gdn_sample(c_raw, conv_buf, conv_w, a_log_l, dt_bias_l, state):
    b = c_raw.shape[0]
    taps = jnp.concatenate([conv_buf, c_raw[:, None, 0:CONV_DIM]], axis=1)
    qk = taps[:, :, 0:2 * W_C].reshape(b, CONV_W, 2 * H_C, DK_C, 1)
    vv = taps[:, :, 2 * W_C:].reshape(b, CONV_W, H_C, 1, DV_C)
    wqk = conv_w[:, 0:2 * W_C].reshape(CONV_W, 2 * H_C, DK_C, 1)
    wv = conv_w[:, 2 * W_C:].reshape(CONV_W, H_C, 1, DV_C)
    bc = c_raw[:, CONV_DIM + W_C:CONV_DIM + W_C + H_C].reshape(b, H_C, 1, 1)
    ac = c_raw[:, CONV_DIM + W_C + H_C:CONV_DIM + W_C + 2 * H_C].reshape(b, H_C, 1, 1)
    per_seq = lambda shp: pl.BlockSpec((1,) + shp, lambda i: (i,) + (0,) * len(shp))
    const = lambda shp: pl.BlockSpec(shp, lambda i: (0,) * len(shp))
    o, s_new = pl.pallas_call(
        _gdn_sample_kernel,
        grid=(b,),
        in_specs=[per_seq((CONV_W, 2 * H_C, DK_C, 1)), per_seq((CONV_W, H_C, 1, DV_C)),
                  const((CONV_W, 2 * H_C, DK_C, 1)), const((CONV_W, H_C, 1, DV_C)),
                  per_seq((H_C, 1, 1)), per_seq((H_C, 1, 1)), const((H_C, 1, 1)), const((H_C, 1, 1)),
                  per_seq((H_C, DK_C, DV_C))],
        out_specs=[per_seq((H_C, 1, DV_C)), per_seq((H_C, DK_C, DV_C))],
        out_shape=[jax.ShapeDtypeStruct((b, H_C, 1, DV_C), F32),
                   jax.ShapeDtypeStruct((b, H_C, DK_C, DV_C), F32)],
        compiler_params=_cp("parallel"),
        name="gdn_sample",
    )(qk, vv, wqk, wv, bc, ac, a_log_l.reshape(H_C, 1, 1), dt_bias_l.reshape(H_C, 1, 1), state)
    return o.reshape(b, W_C), s_new, taps[:, 1:, :]


def _merge_kernel(x_ref, g_ref, oa_ref, ob_ref, oc_ref, gb_ref, zc_ref, gh_ref, gg_ref,
                  wa_ref, wb_ref, wc_ref, wo_ref, o_ref):
    ob = ob_ref[...]
    parts = []
    for h in range(H_B):
        seg = ob[:, h * DV_B:(h + 1) * DV_B]
        parts.append(seg * lax.rsqrt(jnp.mean(seg * seg, axis=-1, keepdims=True) + EPS))
    obn = jnp.concatenate(parts, axis=1) * gh_ref[...] * _silu(gb_ref[...])
    oc = oc_ref[...]
    ms = _dot_exact(oc * oc, _head_sum_matrix()) * (1.0 / DV_C)
    ocn = oc * lax.rsqrt(ms + EPS) * gg_ref[...] * _silu(zc_ref[...])
    m = (jax.nn.sigmoid(g_ref[:, 0:D_MODEL]) * _dot(oa_ref[...].astype(BF16), wa_ref[...])
         + jax.nn.sigmoid(g_ref[:, D_MODEL:2 * D_MODEL]) * _dot(obn.astype(BF16), wb_ref[...])
         + jax.nn.sigmoid(g_ref[:, 2 * D_MODEL:3 * D_MODEL]) * _dot(ocn.astype(BF16), wc_ref[...]))
    o_ref[...] = x_ref[...] + _dot(m.astype(BF16), wo_ref[...])


def merge(x, gates, oa, ob, oc, b_raw, c_raw, g_hgrn, g_gdn, wa, wb, wc, wo, tm):
    n = x.shape[0]
    row = lambda w: pl.BlockSpec((tm, w), lambda i: (i, 0))
    const = lambda r, c: pl.BlockSpec((r, c), lambda i: (0, 0))
    return pl.pallas_call(
        _merge_kernel,
        grid=(n // tm,),
        in_specs=[row(D_MODEL), row(WG_G), row(W_A), row(W_B), row(W_C),
                  pl.BlockSpec((tm, W_B), lambda i: (i, 3)),
                  pl.BlockSpec((tm, W_C), lambda i: (i, 3)),
                  const(1, W_B), const(1, W_C),
                  const(W_A, D_MODEL), const(W_B, D_MODEL), const(W_C, D_MODEL), const(D_MODEL, D_MODEL)],
        out_specs=row(D_MODEL),
        out_shape=jax.ShapeDtypeStruct((n, D_MODEL), F32),
        compiler_params=_cp("parallel"),
        name="merge",
    )(x, gates, oa, ob, oc, b_raw, c_raw, g_hgrn.reshape(1, W_B), g_gdn.reshape(1, W_C), wa, wb, wc, wo)


def _cross_prompt_kernel(x_ref, g_ref, wq_ref, mk_ref, mv_ref, wo_ref, o_ref):
    x = x_ref[...]
    q = _dot(_rms_rows(x, g_ref[...]).astype(BF16), wq_ref[...])
    outs = []
    for h in range(H_X):
        sl = slice(h * HD_X, (h + 1) * HD_X)
        lg = _dot_nt(q[:, sl].astype(BF16), mk_ref[:, sl]) * (HD_X ** -0.5)
        pr = jnp.exp(lg - jnp.max(lg, axis=-1, keepdims=True))
        pr = pr / jnp.sum(pr, axis=-1, keepdims=True)
        outs.append(_dot(pr.astype(BF16), mv_ref[:, sl]))
    o_ref[...] = x + _dot(jnp.concatenate(outs, axis=1).astype(BF16), wo_ref[...])


def cross_prompt(x, g, wq, mk, mv, wo, tm):
    b, t, d = x.shape
    const = lambda r, c: pl.BlockSpec((r, c), lambda bi, i: (0, 0))
    mem = pl.BlockSpec((None, N_MEM, W_X), lambda bi, i: (bi, 0, 0))
    return pl.pallas_call(
        _cross_prompt_kernel,
        grid=(b, t // tm),
        in_specs=[pl.BlockSpec((None, tm, d), lambda bi, i: (bi, i, 0)), const(1, d), const(d, W_X),
                  mem, mem, const(W_X, d)],
        out_specs=pl.BlockSpec((None, tm, d), lambda bi, i: (bi, i, 0)),
        out_shape=jax.ShapeDtypeStruct((b, t, d), F32),
        compiler_params=_cp("parallel", "parallel"),
        name="cross_prompt",
    )(x, g.reshape(1, d), wq, mk, mv, wo)


def _cross_sample_kernel(q_ref, mk_ref, mv_ref, o_ref):
    q = q_ref[0].astype(BF16).astype(F32)
    mk = mk_ref[0].astype(BF16).astype(F32)
    mv = mv_ref[0].astype(BF16).astype(F32)
    outs = []
    for h in range(H_X):
        sl = slice(h * HD_X, (h + 1) * HD_X)
        lg = jnp.sum(mk[:, sl] * q[:, sl], axis=1, keepdims=True) * (HD_X ** -0.5)
        pr = jnp.exp(lg - jnp.max(lg, axis=0, keepdims=True))
        pr = (pr / jnp.sum(pr, axis=0, keepdims=True)).astype(BF16).astype(F32)
        outs.append(jnp.sum(pr * mv[:, sl], axis=0, keepdims=True))
    o_ref[0] = jnp.concatenate(outs, axis=1)


def cross_sample(q, mk, mv):
    b = q.shape[0]
    mem = pl.BlockSpec((1, N_MEM, W_X), lambda i: (i, 0, 0))
    row = pl.BlockSpec((1, 1, W_X), lambda i: (i, 0, 0))
    return pl.pallas_call(
        _cross_sample_kernel,
        grid=(b,),
        in_specs=[row, mem, mem],
        out_specs=row,
        out_shape=jax.ShapeDtypeStruct((b, 1, W_X), F32),
        compiler_params=_cp("parallel"),
        name="cross_sample",
    )(q.reshape(b, 1, W_X), mk, mv).reshape(b, W_X)


def _group_weights(w_in):
    pad = lambda a, w: jnp.pad(a, ((0, 0), (0, 0), (0, w - a.shape[-1]))).astype(BF16)
    wa = pad(w_in[:, :, OFFS[0]:OFFS[6]], WG_A)
    wb = w_in[:, :, OFFS[6]:OFFS[10]].astype(BF16)
    wc = pad(w_in[:, :, OFFS[10]:OFFS[14]], WG_C)
    wg = w_in[:, :, OFFS[14]:OFFS[15]].astype(BF16)
    return wa, wb, wc, wg


def _rope_tables(pos):
    half = HD_A // 2
    inv = ROPE_THETA ** (-jnp.arange(half, dtype=F32) / half)
    ang = pos.astype(F32)[:, None] * inv[None, :]
    cos = jnp.tile(jnp.concatenate([jnp.cos(ang), jnp.cos(ang)], axis=1), (1, H_A))
    sin = jnp.tile(jnp.concatenate([-jnp.sin(ang), jnp.sin(ang)], axis=1), (1, H_A))
    return cos, sin


def _lower_bounds(lb_param):
    p = jax.nn.softmax(lb_param.astype(F32), axis=0)
    c = jnp.cumsum(p, axis=0)
    return c - c[0:1]


def kernel(x_prompt, x_sample, cache_attn_k, cache_attn_v, cache_idx_k, cache_mem_k, cache_mem_v, state_hgrn, state_gdn, state_conv, page_table, mem_prompt, norm_mix, w_in, g_kidx, lb_param, g_hgrn, conv_w, a_log, dt_bias, g_gdn, w_br_a, w_br_b, w_br_c, w_out, norm_cross, norm_mem, w_xq, w_xk, w_xv, w_xo, norm_mlp, w_up, w_down, norm_final):
    bp, tp, d = x_prompt.shape
    bs = x_sample.shape[0]
    n_pool = cache_attn_k.shape[1]
    past = page_table.shape[1] * PAGE
    npr = bp * tp
    tm_p = 512

    wa, wb, wc, wg = _group_weights(w_in)
    bf = lambda a: a.astype(BF16)
    w_br_a, w_br_b, w_br_c, w_out = bf(w_br_a), bf(w_br_b), bf(w_br_c), bf(w_out)
    w_xq, w_xk, w_xv, w_xo, w_up, w_down = bf(w_xq), bf(w_xk), bf(w_xv), bf(w_xo), bf(w_up), bf(w_down)
    lb = _lower_bounds(lb_param)
    gk = jnp.pad(g_kidx, ((0, 0), (0, LANES - D_IDX)))
    a_log_l = jnp.repeat(a_log, DK_C, axis=1)
    dt_bias_l = jnp.repeat(dt_bias, DK_C, axis=1)
    cos_p, sin_p = _rope_tables(jnp.arange(tp))
    cos_s, sin_s = _rope_tables(jnp.full((bs,), past))
    pool_k = cache_attn_k.reshape(DEPTH, n_pool, PAGE, W_A)
    pool_v = cache_attn_v.reshape(DEPTH, n_pool, PAGE, W_A)
    mem_flat = mem_prompt.reshape(bp * N_MEM, d)
    zero_buf = jnp.zeros((bp, 8, CONV_DIM), F32)

    xp = x_prompt.reshape(npr, d)
    xs = x_sample.reshape(bs, d)
    outs = {k: [] for k in ("kp", "vp", "ip", "mkp", "mvp", "hp", "gp", "cp", "ks", "vs", "is", "hs", "gs", "cs")}
    for l in range(DEPTH):
        a_raw = norm_linear(xp, norm_mix[l], wa[l], tm_p)
        b_raw = norm_linear(xp, norm_mix[l], wb[l], tm_p)
        c_raw = norm_linear(xp, norm_mix[l], wc[l], tm_p)
        gates = norm_linear(xp, norm_mix[l], wg[l], tm_p)
        q, k, kb, vb, qi, tail, ki2 = dsa_prep(a_raw, cos_p, sin_p, gk[l:l + 1], tm_p, tp // tm_p)
        r3 = lambda a: a.reshape(bp, tp, a.shape[-1])
        oa = dsa_prompt(r3(q), r3(qi), r3(tail), r3(ki2), r3(kb), r3(vb)).reshape(npr, W_A)
        ob, sh = hgrn_prompt(r3(b_raw), lb[l])
        qc, kc, vc, eg, be = gdn_prep(r3(c_raw), zero_buf, conv_w[l], a_log_l[l:l + 1], dt_bias_l[l:l + 1], tm_p)
        oc, sg = gdn_prompt(qc, kc, vc, eg, be)
        xp = merge(xp, gates, oa, ob.reshape(npr, W_B), oc.reshape(npr, W_C), b_raw, c_raw,
                   g_hgrn[l], g_gdn[l], w_br_a[l], w_br_b[l], w_br_c[l], w_out[l], tm_p)
        mk = norm_linear(mem_flat, norm_mem[l], w_xk[l], N_MEM)
        mv = norm_linear(mem_flat, norm_mem[l], w_xv[l], N_MEM)
        xp = cross_prompt(xp.reshape(bp, tp, d), norm_cross[l], w_xq[l], bf(mk).reshape(bp, N_MEM, W_X),
                          bf(mv).reshape(bp, N_MEM, W_X), w_xo[l], tm_p).reshape(npr, d)
        xp = mlp(xp, norm_mlp[l], w_up[l], w_down[l], 1024, 1024)
        outs["kp"].append(k.reshape(bp, tp, H_A, HD_A))
        outs["vp"].append(a_raw[:, 2 * W_A:3 * W_A].reshape(bp, tp, H_A, HD_A))
        outs["ip"].append(tail[:, 0:D_IDX].reshape(bp, tp, D_IDX))
        outs["mkp"].append(mk.reshape(bp, N_MEM, H_X, HD_X))
        outs["mvp"].append(mv.reshape(bp, N_MEM, H_X, HD_X))
        outs["hp"].append(sh)
        outs["gp"].append(sg)
        outs["cp"].append(r3(c_raw)[:, tp - (CONV_W - 1):, 0:CONV_DIM])

        a_raw = norm_linear(xs, norm_mix[l], wa[l], bs)
        b_raw = norm_linear(xs, norm_mix[l], wb[l], bs)
        c_raw = norm_linear(xs, norm_mix[l], wc[l], bs)
        gates = norm_linear(xs, norm_mix[l], wg[l], bs)
        q, k, kb, vb, qi, tail, ki2 = dsa_prep(a_raw, cos_s, sin_s, gk[l:l + 1], bs, 1)
        va = a_raw[:, 2 * W_A:3 * W_A]
        sc = dsa_sample_scores(page_table, qi, tail, cache_idx_k[l]).reshape(bs, past)
        bias, new_sel = dsa_sample_select(sc, qi, tail)
        oa = dsa_sample_attend(page_table, q, k, va, bias, new_sel, pool_k[l], pool_v[l]).reshape(bs, W_A)
        ob, sh = hgrn_sample(b_raw, lb[l], state_hgrn[l])
        oc, sg, conv_new = gdn_sample(c_raw, state_conv[l], conv_w[l], a_log[l], dt_bias[l], state_gdn[l])
        xs = merge(xs, gates, oa, ob, oc, b_raw, c_raw,
                   g_hgrn[l], g_gdn[l], w_br_a[l], w_br_b[l], w_br_c[l], w_out[l], bs)
        qx = norm_linear(xs, norm_cross[l], w_xq[l], bs)
        ox = cross_sample(qx, cache_mem_k[l].reshape(bs, N_MEM, W_X), cache_mem_v[l].reshape(bs, N_MEM, W_X))
        xs = linear_residual(xs, ox, w_xo[l], bs)
        xs = mlp(xs, norm_mlp[l], w_up[l], w_down[l], bs, 1024)
        outs["ks"].append(k.reshape(bs, 1, H_A, HD_A))
        outs["vs"].append(va.reshape(bs, 1, H_A, HD_A))
        outs["is"].append(tail[:, 0:D_IDX].reshape(bs, 1, D_IDX))
        outs["hs"].append(sh)
        outs["gs"].append(sg)
        outs["cs"].append(conv_new)

    y_prompt = final_norm(xp, norm_final, tm_p).reshape(bp, tp, d)
    y_sample = final_norm(xs, norm_final, bs).reshape(bs, 1, d)
    st = jnp.stack
    return (y_prompt, y_sample,
            st(outs["kp"]), st(outs["vp"]), st(outs["ip"]), st(outs["mkp"]), st(outs["mvp"]),
            st(outs["hp"]), st(outs["gp"]), st(outs["cp"]),
            st(outs["ks"]), st(outs["vs"]), st(outs["is"]), st(outs["hs"]), st(outs["gs"]), st(outs["cs"]))
```

```python
import functools
import math

import jax
import jax.numpy as jnp
import numpy as np
from jax import lax
from jax.experimental import pallas as pl
from jax.experimental.pallas import tpu as pltpu

F32 = jnp.float32
BF16 = jnp.bfloat16
I32 = jnp.int32

D_MODEL = 1024
DEPTH = 4
PAGE = 128
H_A, HD_A = 8, 64
H_IDX, D_IDX = 8, 64
TOPK = 256
ROPE_THETA = 10000.0
H_B, DK_B, DV_B = 4, 128, 128
H_C, DK_C, DV_C = 8, 64, 64
CONV_W = 4
N_MEM = 256
H_X, HD_X = 4, 128
D_FF = 4 * D_MODEL
EPS = 1e-6

W_A = H_A * HD_A
W_B = H_B * DK_B
W_C = H_C * DK_C
CONV_DIM = 3 * W_C
W_X = H_X * HD_X
SPLITS = (W_A, W_A, W_A, W_A, D_IDX, H_IDX, W_B, W_B, W_B, W_B, CONV_DIM, W_C, H_C, H_C, 3 * D_MODEL)
OFFS = tuple(int(o) for o in np.cumsum((0,) + SPLITS))
WG_A = 2176
WG_B = 2048
WG_C = 2176
WG_G = 3 * D_MODEL

LANES = 128
SUBLANES = 8
VMEM_LIMIT = 56 * 1024 * 1024
NEG_INF = float("-inf")
KEY_NEG_INF = -2139095041
HIGHEST = lax.Precision.HIGHEST

TQ = 128
SC = 256
SUB = 16
TB = 256


def _cp(*sem):
    return pltpu.CompilerParams(dimension_semantics=sem, vmem_limit_bytes=VMEM_LIMIT)


def _rms_rows(x, g):
    return x * lax.rsqrt(jnp.mean(x * x, axis=-1, keepdims=True) + EPS) * g


def _silu(x):
    return x * jax.nn.sigmoid(x)


def _dot(a, b):
    return jnp.dot(a, b, preferred_element_type=F32)


def _dot_nt(a, b):
    return lax.dot_general(a, b, (((1,), (1,)), ((), ())), preferred_element_type=F32)


def _dot_exact(a, b):
    return jnp.dot(a, b, preferred_element_type=F32, precision=HIGHEST)


def _norm_linear_kernel(x_ref, g_ref, w_ref, o_ref):
    h = _rms_rows(x_ref[...], g_ref[...])
    o_ref[...] = _dot(h.astype(BF16), w_ref[...])


def norm_linear(x, g, w, tm):
    n, d = x.shape
    wd = w.shape[1]
    return pl.pallas_call(
        _norm_linear_kernel,
        grid=(n // tm,),
        in_specs=[pl.BlockSpec((tm, d), lambda i: (i, 0)),
                  pl.BlockSpec((1, d), lambda i: (0, 0)),
                  pl.BlockSpec((d, wd), lambda i: (0, 0))],
        out_specs=pl.BlockSpec((tm, wd), lambda i: (i, 0)),
        out_shape=jax.ShapeDtypeStruct((n, wd), F32),
        compiler_params=_cp("parallel"),
        name="norm_linear",
    )(x, g.reshape(1, d), w)


def _linear_residual_kernel(x_ref, a_ref, w_ref, o_ref):
    o_ref[...] = x_ref[...] + _dot(a_ref[...].astype(BF16), w_ref[...])


def linear_residual(x, a, w, tm):
    n, d = x.shape
    k = a.shape[1]
    return pl.pallas_call(
        _linear_residual_kernel,
        grid=(n // tm,),
        in_specs=[pl.BlockSpec((tm, d), lambda i: (i, 0)),
                  pl.BlockSpec((tm, k), lambda i: (i, 0)),
                  pl.BlockSpec((k, d), lambda i: (0, 0))],
        out_specs=pl.BlockSpec((tm, d), lambda i: (i, 0)),
        out_shape=jax.ShapeDtypeStruct((n, d), F32),
        compiler_params=_cp("parallel"),
        name="linear_residual",
    )(x, a, w)


def _mlp_kernel(x_ref, g_ref, wu_ref, wd_ref, o_ref, h_scr):
    j = pl.program_id(1)

    @pl.when(j == 0)
    def _():
        x = x_ref[...]
        h_scr[...] = _rms_rows(x, g_ref[...]).astype(BF16)
        o_ref[...] = x

    u = _dot(h_scr[...], wu_ref[...])
    a = jnp.square(jnp.maximum(u, 0.0)).astype(BF16)
    o_ref[...] += _dot(a, wd_ref[...])


def mlp(x, g, w_up, w_down, tm, tf):
    n, d = x.shape
    ff = w_up.shape[1]
    return pl.pallas_call(
        _mlp_kernel,
        grid=(n // tm, ff // tf),
        in_specs=[pl.BlockSpec((tm, d), lambda i, j: (i, 0)),
                  pl.BlockSpec((1, d), lambda i, j: (0, 0)),
                  pl.BlockSpec((d, tf), lambda i, j: (0, j)),
                  pl.BlockSpec((tf, d), lambda i, j: (j, 0))],
        out_specs=pl.BlockSpec((tm, d), lambda i, j: (i, 0)),
        out_shape=jax.ShapeDtypeStruct((n, d), F32),
        scratch_shapes=[pltpu.VMEM((tm, d), BF16)],
        compiler_params=_cp("parallel", "arbitrary"),
        name="mlp",
    )(x, g.reshape(1, d), w_up, w_down)


def _final_norm_kernel(x_ref, g_ref, o_ref):
    o_ref[...] = _rms_rows(x_ref[...], g_ref[...])


def final_norm(x, g, tm):
    n, d = x.shape
    return pl.pallas_call(
        _final_norm_kernel,
        grid=(n // tm,),
        in_specs=[pl.BlockSpec((tm, d), lambda i: (i, 0)), pl.BlockSpec((1, d), lambda i: (0, 0))],
        out_specs=pl.BlockSpec((tm, d), lambda i: (i, 0)),
        out_shape=jax.ShapeDtypeStruct((n, d), F32),
        compiler_params=_cp("parallel"),
        name="final_norm",
    )(x, g.reshape(1, d))


def _rope(x, c, s):
    w = x.shape[-1]
    lane = lax.broadcasted_iota(I32, x.shape, 1)
    first = (lane % HD_A) < (HD_A // 2)
    partner = jnp.where(first, pltpu.roll(x, w - HD_A // 2, 1), pltpu.roll(x, HD_A // 2, 1))
    return x * c + partner * s


def _dsa_prep_kernel(a_ref, c_ref, s_ref, gk_ref, q_ref, k_ref, kb_ref, qi_ref, tail_ref, ki2_ref):
    c = c_ref[...]
    s = s_ref[...]
    qa = a_ref[:, 0:W_A]
    ka = a_ref[:, W_A:2 * W_A]
    qi = a_ref[:, 3 * W_A:4 * W_A]
    t = a_ref[:, 4 * W_A:4 * W_A + LANES]
    q_ref[...] = (_rope(qa, c, s) * (HD_A ** -0.5)).astype(BF16)
    kr = _rope(ka, c, s)
    k_ref[...] = kr
    kb_ref[...] = kr.astype(BF16)
    qi_ref[...] = _rope(qi, c, s).astype(BF16)
    lane = lax.broadcasted_iota(I32, t.shape, 1)
    is_k = lane < D_IDX
    ms = jnp.sum(jnp.where(is_k, t * t, 0.0), axis=-1, keepdims=True) * (1.0 / D_IDX)
    kin = t * lax.rsqrt(ms + EPS) * gk_ref[...]
    kir = _rope(kin, c[:, :LANES], s[:, :LANES])
    is_w = jnp.logical_and(lane >= D_IDX, lane < D_IDX + H_IDX)
    tail_ref[...] = kir + jnp.where(is_w, t * (H_IDX ** -0.5 * D_IDX ** -0.5), 0.0)
    ki2_ref[...] = (kir + pltpu.roll(kir, D_IDX, 1)).astype(BF16)


def dsa_prep(a_raw, cos, sin, gk, tm, n_tab_tiles):
    n = a_raw.shape[0]
    row = lambda w: pl.BlockSpec((tm, w), lambda i: (i, 0))
    tab = pl.BlockSpec((tm, W_A), lambda i: (i % n_tab_tiles, 0))
    sds = lambda w, dt: jax.ShapeDtypeStruct((n, w), dt)
    return pl.pallas_call(
        _dsa_prep_kernel,
        grid=(n // tm,),
        in_specs=[row(WG_A), tab, tab, pl.BlockSpec((1, LANES), lambda i: (0, 0))],
        out_specs=[row(W_A), row(W_A), row(W_A), row(W_A), row(LANES), row(LANES)],
        out_shape=[sds(W_A, BF16), sds(W_A, F32), sds(W_A, BF16), sds(W_A, BF16),
                   sds(LANES, F32), sds(LANES, BF16)],
        compiler_params=_cp("parallel"),
        name="dsa_prep",
    )(a_raw, cos, sin, gk)


def _v_chunks_kernel(a_ref, o_ref):
    o_ref[0] = a_ref[...].T.astype(BF16)


def v_chunks_t(a_raw):
    n = a_raw.shape[0]
    return pl.pallas_call(
        _v_chunks_kernel,
        grid=(n // SC,),
        in_specs=[pl.BlockSpec((SC, W_A), lambda i: (i, 2))],
        out_specs=pl.BlockSpec((1, W_A, SC), lambda i: (i, 0, 0)),
        out_shape=jax.ShapeDtypeStruct((n // SC, W_A, SC), BF16),
        compiler_params=_cp("parallel"),
        name="v_chunks_t",
    )(a_raw)


def _sort_key(x):
    i = pltpu.bitcast(x, I32)
    return i ^ (lax.shift_right_arithmetic(i, 31) & 0x7FFFFFFF)


def _kth_largest_key(count_ge, shape):
    def cond(s):
        return jnp.logical_and(s[0] < 32, s[3] == 0)

    def body(s):
        b, t, done, _ = s
        cand = t + lax.shift_left(jnp.int32(1), 31 - b)
        n = count_ge(cand)
        done = jnp.where(n == TOPK, 1, done)
        return b + 1, jnp.where(n >= TOPK, cand, t), done, jnp.min(done)

    init = (jnp.int32(0), jnp.full(shape, -2 ** 31, I32), jnp.zeros(shape, I32), jnp.int32(0))
    return lax.while_loop(cond, body, init)[1]


def _dsa_prompt_kernel(q_ref, qi_ref, tail_ref, ki2_ref, k_ref, vt_ref, o_ref,
                       qis_scr, qm_scr, key_scr, acc_scr):
    i = pl.program_id(1)
    t0 = i * TQ
    nchunk = (t0 + TQ + SC - 1) // SC
    lane = lax.broadcasted_iota(I32, (TQ, LANES), 1)
    half = lane // HD_A

    for h in range(H_A):
        p, hl = h // 2, h % 2
        sl = slice(p * LANES, (p + 1) * LANES)
        qis_scr[h * TQ:(h + 1) * TQ, :] = jnp.where(half == hl, qi_ref[:, sl].astype(F32), 0.0).astype(BF16)
        qm_scr[h * TQ:(h + 1) * TQ, :] = jnp.where(half == hl, q_ref[:, sl].astype(F32), 0.0).astype(BF16)
    w_t = tail_ref[...].T

    key_s = lax.broadcasted_iota(I32, (SC, TQ), 0)
    qry_t = t0 + lax.broadcasted_iota(I32, (SC, TQ), 1)

    def score_chunk(c, carry):
        base = pl.multiple_of(c * SC, SC)
        s_all = _dot_nt(ki2_ref[pl.ds(base, SC), :], qis_scr[...])
        acc = jnp.zeros((SC, TQ), F32)
        for h in range(H_IDX):
            acc = acc + w_t[D_IDX + h:D_IDX + h + 1, :] * jnp.maximum(s_all[:, h * TQ:(h + 1) * TQ], 0.0)
        sc = jnp.where(key_s + base <= qry_t, acc, NEG_INF)
        key_scr[c] = _sort_key(sc)
        return carry

    lax.fori_loop(0, nchunk, score_chunk, 0)

    def count(pred):
        def body(c, part):
            hit = jnp.where(pred(key_scr[c], c), 1, 0)
            return part + jnp.sum(hit.reshape(SC // SUBLANES, SUBLANES, TQ), axis=0)
        part = lax.fori_loop(0, nchunk, body, jnp.zeros((SUBLANES, TQ), I32))
        return jnp.sum(part, axis=0, keepdims=True)

    thr = _kth_largest_key(lambda cand: count(lambda kx, c: kx >= cand), (1, TQ))
    n_ge = count(lambda kx, c: kx >= thr)
    excess = jnp.logical_and(n_ge > TOPK, thr > KEY_NEG_INF)

    @pl.when(jnp.max(excess.astype(I32)) > 0)
    def _():
        need = TOPK - count(lambda kx, c: kx > thr)

        def body(b, y):
            cand = y + lax.shift_left(jnp.int32(1), 13 - b)
            f = count(lambda kx, c: jnp.logical_and(kx == thr, key_s + c * SC < cand))
            return jnp.where(f < need, cand, y)

        cut = lax.fori_loop(0, 14, body, jnp.zeros((1, TQ), I32)) + 1
        cut = jnp.where(excess, cut, jnp.int32(2 ** 30))

        def drop(c, carry):
            kx = key_scr[c]
            dropped = jnp.logical_and(kx == thr, key_s + c * SC >= cut)
            key_scr[c] = jnp.where(dropped, KEY_NEG_INF, kx)
            return carry

        lax.fori_loop(0, nchunk, drop, 0)

    def to_bias(c, carry):
        sel = jnp.logical_and(key_scr[c] >= thr, key_s + c * SC <= qry_t)
        key_scr[c] = pltpu.bitcast(jnp.where(sel, 0.0, NEG_INF), I32)
        return carry

    lax.fori_loop(0, nchunk, to_bias, 0)

    npair = H_A // 2
    acc_scr[...] = jnp.zeros(acc_scr.shape, F32)

    def attend_chunk(c, carry):
        base = pl.multiple_of(c * SC, SC)
        bias = pltpu.bitcast(key_scr[c], F32)
        bias2 = jnp.concatenate([bias, bias], axis=1)
        pairs = [slice(p * LANES, (p + 1) * LANES) for p in range(npair)]
        lgs = [_dot_nt(k_ref[pl.ds(base, SC), pairs[p]], qm_scr[2 * p * TQ:(2 * p + 2) * TQ, :])
               for p in range(npair)]
        new, prs, alphas = [], [], []
        for p in range(npair):
            m_old, l_old = carry[p]
            x = lgs[p] + bias2
            m_new = jnp.maximum(m_old, jnp.max(x, axis=0, keepdims=True))
            m_safe = jnp.where(m_new == NEG_INF, 0.0, m_new)
            pr = jnp.exp(x - m_safe)
            alpha = jnp.exp(m_old - m_safe)
            new.append((m_new, alpha * l_old + jnp.sum(pr, axis=0, keepdims=True)))
            prs.append(pr.astype(BF16))
            alphas.append(alpha)
        pvs = [_dot(vt_ref[c, pairs[p], :], prs[p]) for p in range(npair)]
        for p in range(npair):
            acc_scr[p] = alphas[p] * acc_scr[p] + pvs[p]
        return tuple(new)

    init = tuple((jnp.full((1, 2 * TQ), NEG_INF, F32), jnp.zeros((1, 2 * TQ), F32)) for _ in range(npair))
    fin = lax.fori_loop(0, nchunk, attend_chunk, init)
    for p in range(npair):
        l_fin = fin[p][1]
        acc = acc_scr[p]
        o_t = jnp.concatenate([acc[0:HD_A, 0:TQ] / l_fin[:, 0:TQ],
                               acc[HD_A:2 * HD_A, TQ:2 * TQ] / l_fin[:, TQ:2 * TQ]], axis=0)
        o_ref[:, p * LANES:(p + 1) * LANES] = o_t.T


def dsa_prompt(q, qi, tail, ki2, kb, vt):
    b, t, _ = q.shape
    blk = lambda w: pl.BlockSpec((None, TQ, w), lambda bi, i: (bi, i, 0))
    full = lambda w: pl.BlockSpec((None, t, w), lambda bi, i: (bi, 0, 0), pipeline_mode=pl.Buffered(1))
    return pl.pallas_call(
        _dsa_prompt_kernel,
        grid=(b, t // TQ),
        in_specs=[blk(W_A), blk(W_A), blk(LANES), full(LANES), full(W_A),
                  pl.BlockSpec((None, t // SC, W_A, SC), lambda bi, i: (bi, 0, 0, 0),
                               pipeline_mode=pl.Buffered(1))],
        out_specs=blk(W_A),
        out_shape=jax.ShapeDtypeStruct((b, t, W_A), F32),
        scratch_shapes=[pltpu.VMEM((H_IDX * TQ, LANES), BF16),
                        pltpu.VMEM((H_A * TQ, LANES), BF16),
                        pltpu.VMEM((t // SC, SC, TQ), I32),
                        pltpu.VMEM((H_A // 2, LANES, 2 * TQ), F32)],
        compiler_params=_cp("parallel", "arbitrary"),
        name="dsa_prompt",
    )(q, qi, tail, ki2, kb, vt)


PG = 8


def _head_rows(row, n_heads, width):
    full = jnp.broadcast_to(row, (n_heads, n_heads * width))
    hh = lax.broadcasted_iota(I32, full.shape, 0)
    ll = lax.broadcasted_iota(I32, full.shape, 1) // width
    return jnp.where(hh == ll, full, 0.0)


def _dsa_sample_score_kernel(pt_ref, qi_ref, w_ref, *refs):
    page_refs, o_ref = refs[:PG], refs[PG]
    j = pl.program_id(1)
    qh = qi_ref[0]
    w_col = w_ref[0]
    for g in range(PG):
        kp = page_refs[g][...].astype(BF16)
        s = jnp.maximum(_dot_nt(qh, kp), 0.0)
        o_ref[0, :, pl.ds(pl.multiple_of((j * PG + g) * PAGE, PAGE), PAGE)] = jnp.sum(
            w_col * s, axis=0, keepdims=True)


def dsa_sample_scores(page_table, layer, qi, tail, pool_kidx):
    b, n_pages = page_table.shape
    page_spec = lambda g: pl.BlockSpec((None, None, PAGE, D_IDX),
                                       lambda bi, j, pt: (layer, pt[bi, j * PG + g], 0, 0))
    return pl.pallas_call(
        _dsa_sample_score_kernel,
        grid_spec=pltpu.PrefetchScalarGridSpec(
            num_scalar_prefetch=1,
            grid=(b, n_pages // PG),
            in_specs=[pl.BlockSpec((1, H_IDX, D_IDX), lambda bi, j, pt: (bi, 0, 0)),
                      pl.BlockSpec((1, H_IDX, 1), lambda bi, j, pt: (bi, 0, 0))]
                     + [page_spec(g) for g in range(PG)],
            out_specs=pl.BlockSpec((1, 1, n_pages * PAGE), lambda bi, j, pt: (bi, 0, 0)),
        ),
        out_shape=jax.ShapeDtypeStruct((b, 1, n_pages * PAGE), F32),
        compiler_params=_cp("parallel", "arbitrary"),
        name="dsa_sample_scores",
    )(page_table, qi.reshape(b, H_IDX, D_IDX), tail[:, D_IDX:D_IDX + H_IDX].reshape(b, H_IDX, 1),
      *([pool_kidx] * PG))


def _dsa_sample_select_kernel(sc_ref, qi_ref, tail_ref, bias_ref, newsel_ref):
    sc = sc_ref[...]
    nb, past = sc.shape
    qi = qi_ref[...]
    ki = tail_ref[...]
    lane = lax.broadcasted_iota(I32, ki.shape, 1)
    kidx = jnp.where(lane < D_IDX, ki, 0.0)
    kidx = (kidx + pltpu.roll(kidx, D_IDX, 1)).astype(BF16).astype(F32)
    sc_new = jnp.zeros((nb, 1), F32)
    for h in range(H_IDX):
        p, hl = h // 2, h % 2
        qh = qi[:, p * LANES:(p + 1) * LANES].astype(F32)
        d = jnp.sum(jnp.where(lane // D_IDX == hl, qh * kidx, 0.0), axis=1, keepdims=True)
        w = jnp.sum(jnp.where(lane == D_IDX + h, ki, 0.0), axis=1, keepdims=True)
        sc_new = sc_new + w * jnp.maximum(d, 0.0)
    key = _sort_key(sc + 0.0)
    key_new = _sort_key(sc_new + 0.0)

    def count_ge(cand):
        return (jnp.sum(jnp.where(key >= cand, 1, 0), axis=1, keepdims=True)
                + jnp.where(key_new >= cand, 1, 0))

    thr = _kth_largest_key(count_ge, (nb, 1))
    n_gt = jnp.sum(jnp.where(key > thr, 1, 0), axis=1, keepdims=True) + jnp.where(key_new > thr, 1, 0)
    need = TOPK - n_gt
    eq = key == thr
    pos = lax.broadcasted_iota(I32, sc.shape, 1)

    def body(b, y):
        cand = y + lax.shift_left(jnp.int32(1), 13 - b)
        f = jnp.sum(jnp.where(jnp.logical_and(eq, pos < cand), 1, 0), axis=1, keepdims=True)
        return jnp.where(f < need, cand, y)

    cut = lax.fori_loop(0, 14, body, jnp.zeros((nb, 1), I32)) + 1
    sel = jnp.logical_or(key > thr, jnp.logical_and(eq, pos < cut))
    n_sel = jnp.sum(jnp.where(sel, 1, 0), axis=1, keepdims=True)
    bias_ref[...] = jnp.where(sel, 0.0, NEG_INF)
    new_sel = jnp.logical_or(key_new > thr, jnp.logical_and(key_new == thr, n_sel < TOPK))
    newsel_ref[...] = jnp.broadcast_to(jnp.where(new_sel, 0.0, NEG_INF), newsel_ref.shape)


def dsa_sample_select(sc, qi, tail):
    b, past = sc.shape
    return pl.pallas_call(
        _dsa_sample_select_kernel,
        out_shape=[jax.ShapeDtypeStruct((b, past), F32), jax.ShapeDtypeStruct((b, LANES), F32)],
        compiler_params=pltpu.CompilerParams(vmem_limit_bytes=VMEM_LIMIT),
        name="dsa_sample_select",
    )(sc, qi, tail)


def _dsa_sample_attn_kernel(pt_ref, q_ref, kn_ref, vn_ref, bias_ref, nsel_ref, *refs):
    k_refs, v_refs, o_ref = refs[:PG], refs[PG:2 * PG], refs[2 * PG]
    m_scr, l_scr, acc_scr = refs[2 * PG + 1:]
    j = pl.program_id(1)
    qh = q_ref[0]
    own = (lax.broadcasted_iota(I32, (H_A, PAGE * H_A), 1) % H_A
           == lax.broadcasted_iota(I32, (H_A, PAGE * H_A), 0))

    @pl.when(j == 0)
    def _():
        lg = jnp.sum(qh.astype(F32) * kn_ref[0].astype(BF16).astype(F32), axis=1, keepdims=True)
        lg = lg + nsel_ref[0][:, 0:1]
        m_scr[...] = lg
        pr = jnp.where(lg == NEG_INF, 0.0, 1.0)
        l_scr[...] = pr
        acc_scr[...] = pr * vn_ref[0].astype(BF16).astype(F32)

    for g in range(PG):
        kp = k_refs[g][...].reshape(PAGE * H_A, HD_A).astype(BF16)
        vp = v_refs[g][...].reshape(PAGE * H_A, HD_A).astype(BF16)
        bias = bias_ref[0, :, pl.ds(pl.multiple_of((j * PG + g) * PAGE * H_A, PAGE * H_A), PAGE * H_A)]
        x = jnp.where(own, _dot_nt(qh, kp) + bias, NEG_INF)
        m_old = m_scr[...]
        m_new = jnp.maximum(m_old, jnp.max(x, axis=1, keepdims=True))
        m_safe = jnp.where(m_new == NEG_INF, 0.0, m_new)
        pr = jnp.exp(x - m_safe)
        alpha = jnp.exp(m_old - m_safe)
        l_scr[...] = alpha * l_scr[...] + jnp.sum(pr, axis=1, keepdims=True)
        acc_scr[...] = alpha * acc_scr[...] + _dot(pr.astype(BF16), vp)
        m_scr[...] = m_new

    @pl.when(j == pl.num_programs(1) - 1)
    def _():
        o_ref[0] = acc_scr[...] / l_scr[...]


def dsa_sample_attend(page_table, layer, q, k_new, v_new, bias, new_sel, cache_k, cache_v):
    b, n_pages = page_table.shape
    page_spec = lambda g: pl.BlockSpec((None, None, PAGE, H_A, HD_A),
                                       lambda bi, j, pt: (layer, pt[bi, j * PG + g], 0, 0, 0))
    heads = pl.BlockSpec((1, H_A, HD_A), lambda bi, j, pt: (bi, 0, 0))
    row = lambda w: pl.BlockSpec((1, 1, w), lambda bi, j, pt: (bi, 0, 0))
    h3 = lambda a: a.reshape(b, H_A, HD_A)
    return pl.pallas_call(
        _dsa_sample_attn_kernel,
        grid_spec=pltpu.PrefetchScalarGridSpec(
            num_scalar_prefetch=1,
            grid=(b, n_pages // PG),
            in_specs=[heads, heads, heads, row(n_pages * PAGE * H_A), row(LANES)]
                     + [page_spec(g) for g in range(PG)] * 2,
            out_specs=heads,
            scratch_shapes=[pltpu.VMEM((H_A, 1), F32), pltpu.VMEM((H_A, 1), F32),
                            pltpu.VMEM((H_A, HD_A), F32)],
        ),
        out_shape=jax.ShapeDtypeStruct((b, H_A, HD_A), F32),
        compiler_params=_cp("parallel", "arbitrary"),
        name="dsa_sample_attend",
    )(page_table, h3(q), h3(k_new), h3(v_new),
      jnp.repeat(bias, H_A, axis=1).reshape(b, 1, n_pages * PAGE * H_A), new_sel.reshape(b, 1, LANES),
      *([cache_k] * PG), *([cache_v] * PG))


def _block_tri(n, blk, strict_upper):
    r = lax.broadcasted_iota(I32, (n, n), 0)
    c = lax.broadcasted_iota(I32, (n, n), 1)
    same = (r // blk) == (c // blk)
    tri = (c > r) if strict_upper else (c <= r)
    return jnp.where(jnp.logical_and(same, tri), 1.0, 0.0).astype(F32)


def _split_dot(x, w):
    hi = x.astype(BF16)
    lo = (x - hi.astype(F32)).astype(BF16)
    return _dot(hi, w) + _dot(lo, w)


def _hgrn_prompt_kernel(b_ref, lb_ref, llb_ref, o_ref, s_ref, st_scr):
    i = pl.program_id(1)
    tm = b_ref.shape[0]
    nsub = tm // SUB

    @pl.when(i == 0)
    def _():
        st_scr[...] = jnp.zeros(st_scr.shape, F32)

    lb = lb_ref[...]
    fb = b_ref[:, W_B:2 * W_B]
    q = _silu(b_ref[:, 0:W_B])
    v = b_ref[:, 2 * W_B:3 * W_B]
    kk = (1.0 - lb) * jax.nn.sigmoid(-fb)
    la = llb_ref[0:1, :]
    lbb = llb_ref[1:2, :] + (jnp.minimum(fb, 0.0) - jnp.log1p(jnp.exp(-jnp.abs(fb))))
    logf = jnp.maximum(la, lbb) + jnp.log1p(jnp.exp(-jnp.abs(la - lbb)))

    g = _dot_exact(_block_tri(tm, SUB, False), logf)
    r = _dot_exact(_block_tri(tm, SUB, True), logf)
    qg = (q * jnp.exp(g)).astype(BF16)
    kg = kk * jnp.exp(r)
    eg_last = jnp.exp(g + r)

    rows = lax.broadcasted_iota(I32, (tm, W_B), 0) % SUB
    hr = lax.broadcasted_iota(I32, (W_B, W_B), 0) // DK_B
    hc = lax.broadcasted_iota(I32, (W_B, W_B), 1) // DK_B
    head_ones = jnp.where(hr == hc, 1.0, 0.0).astype(BF16)
    o = jnp.zeros((tm, W_B), F32)
    for j in range(SUB):
        if j == 0:
            w = q * kk
            vj = v
        else:
            dec = jnp.exp(jnp.where(rows >= j, g - pltpu.roll(g, j, 0), NEG_INF))
            w = q * pltpu.roll(kk, j, 0) * dec
            vj = pltpu.roll(v, j, 0)
        o = o + _split_dot(w, head_ones) * vj

    rowi = lax.broadcasted_iota(I32, (tm, DK_B), 0) // SUB
    for h in range(H_B):
        sl = slice(h * DK_B, (h + 1) * DK_B)
        vt = v[:, sl].T.astype(BF16)
        qg_h = qg[:, sl]
        kg_h = kg[:, sl]
        st = st_scr[h]
        oh = jnp.zeros((tm, DV_B), F32)
        for c in range(nsub):
            inc = rowi == c
            oh = oh + jnp.where(inc, _dot_nt(qg_h, st.astype(BF16)), 0.0)
            st = st * eg_last[c * SUB:c * SUB + 1, sl] + _dot(vt, jnp.where(inc, kg_h, 0.0).astype(BF16))
        st_scr[h] = st
        o_ref[:, sl] = o[:, sl] + oh

    @pl.when(i == pl.num_programs(1) - 1)
    def _():
        for h in range(H_B):
            s_ref[h] = st_scr[h].T


def hgrn_prompt(b_raw, lb):
    b, t, _ = b_raw.shape
    tm = 128
    llb = jnp.stack([jnp.log(lb), jnp.log1p(-lb)])
    return pl.pallas_call(
        _hgrn_prompt_kernel,
        grid=(b, t // tm),
        in_specs=[pl.BlockSpec((None, tm, WG_B), lambda bi, i: (bi, i, 0)),
                  pl.BlockSpec((1, W_B), lambda bi, i: (0, 0)),
                  pl.BlockSpec((2, W_B), lambda bi, i: (0, 0))],
        out_specs=[pl.BlockSpec((None, tm, W_B), lambda bi, i: (bi, i, 0)),
                   pl.BlockSpec((None, H_B, DK_B, DV_B), lambda bi, i: (bi, 0, 0, 0))],
        out_shape=[jax.ShapeDtypeStruct((b, t, W_B), F32),
                   jax.ShapeDtypeStruct((b, H_B, DK_B, DV_B), F32)],
        scratch_shapes=[pltpu.VMEM((H_B, DV_B, DK_B), F32)],
        compiler_params=_cp("parallel", "arbitrary"),
        name="hgrn_prompt",
    )(b_raw, lb.reshape(1, W_B), llb)


def _hgrn_sample_kernel(qb_ref, fb_ref, ib_ref, lb_ref, s_ref, o_ref, sn_ref):
    lb = lb_ref[...]
    fb = fb_ref[0]
    f = lb + (1.0 - lb) * jax.nn.sigmoid(fb)
    kk = (1.0 - lb) * jax.nn.sigmoid(-fb)
    s_new = f * s_ref[0] + kk * ib_ref[0]
    sn_ref[0] = s_new
    o_ref[0] = jnp.sum(_silu(qb_ref[0]) * s_new, axis=1, keepdims=True)


def hgrn_sample(b_raw, lb, state):
    b = b_raw.shape[0]
    col = lambda a: a.reshape(b, H_B, DK_B, 1)
    cspec = pl.BlockSpec((1, H_B, DK_B, 1), lambda i: (i, 0, 0, 0))
    rspec = pl.BlockSpec((1, H_B, 1, DV_B), lambda i: (i, 0, 0, 0))
    sspec = pl.BlockSpec((1, H_B, DK_B, DV_B), lambda i: (i, 0, 0, 0))
    o, s_new = pl.pallas_call(
        _hgrn_sample_kernel,
        grid=(b,),
        in_specs=[cspec, cspec, rspec, pl.BlockSpec((H_B, DK_B, 1), lambda i: (0, 0, 0)), sspec],
        out_specs=[rspec, sspec],
        out_shape=[jax.ShapeDtypeStruct((b, H_B, 1, DV_B), F32),
                   jax.ShapeDtypeStruct((b, H_B, DK_B, DV_B), F32)],
        compiler_params=_cp("parallel"),
        name="hgrn_sample",
    )(col(b_raw[:, 0:W_B]), col(b_raw[:, W_B:2 * W_B]), b_raw[:, 2 * W_B:3 * W_B].reshape(b, H_B, 1, DV_B),
      lb.reshape(H_B, DK_B, 1), state)
    return o.reshape(b, W_B), s_new


def _head_sum_matrix():
    r = lax.broadcasted_iota(I32, (W_C, W_C), 0) // DK_C
    c = lax.broadcasted_iota(I32, (W_C, W_C), 1) // DK_C
    return jnp.where(r == c, 1.0, 0.0).astype(F32)


def _gdn_prep_kernel(c_ref, halo_ref, buf_ref, w_ref, al_ref, dt_ref, kq_ref, v_ref, eg_ref, be_ref):
    i = pl.program_id(1)
    tm = c_ref.shape[0]
    prev = jnp.where(i == 0, buf_ref[...], halo_ref[:, 0:CONV_DIM])
    win = jnp.concatenate([prev, c_ref[:, 0:CONV_DIM]], axis=0)
    y = jnp.zeros((tm, CONV_DIM), F32)
    for tap in range(CONV_W):
        off = 8 - (CONV_W - 1) + tap
        y = y + win[off:off + tm, :] * w_ref[tap:tap + 1, :]
    y = _silu(y)
    ones = _head_sum_matrix()
    qc = y[:, 0:W_C]
    kc = y[:, W_C:2 * W_C]
    qn = qc * lax.rsqrt(_dot_exact(qc * qc, ones) + EPS) * (DK_C ** -0.5)
    kn = kc * lax.rsqrt(_dot_exact(kc * kc, ones) + EPS)
    kbits = pltpu.bitcast(kn.astype(BF16).astype(F32), I32) & jnp.int32(-65536)
    qbits = lax.shift_right_logical(pltpu.bitcast(qn.astype(BF16).astype(F32), I32), 16)
    kq_ref[...] = kbits | qbits
    v_ref[...] = y[:, 2 * W_C:3 * W_C]
    tail = c_ref[:, CONV_DIM + W_C:CONV_DIM + W_C + LANES]
    r = lax.broadcasted_iota(I32, (LANES, W_C), 0)
    cc = lax.broadcasted_iota(I32, (LANES, W_C), 1) // DK_C
    bcl = _dot_exact(tail, jnp.where(r == cc, 1.0, 0.0).astype(F32))
    acl = _dot_exact(tail, jnp.where(r == cc + H_C, 1.0, 0.0).astype(F32))
    be_ref[...] = jax.nn.sigmoid(bcl)
    x = acl + dt_ref[...]
    softplus = jnp.maximum(x, 0.0) + jnp.log1p(jnp.exp(-jnp.abs(x)))
    eg_ref[...] = jnp.exp(-jnp.exp(al_ref[...]) * softplus)


def gdn_prep(c_raw, buf8, conv_w, a_log_l, dt_bias_l, tm):
    b, t, _ = c_raw.shape
    blk = lambda w: pl.BlockSpec((None, tm, w), lambda bi, i: (bi, i, 0))
    sds = jax.ShapeDtypeStruct((b, t, W_C), F32)
    return pl.pallas_call(
        _gdn_prep_kernel,
        grid=(b, t // tm),
        in_specs=[blk(WG_C),
                  pl.BlockSpec((None, 8, WG_C), lambda bi, i: (bi, jnp.maximum(i * (tm // 8) - 1, 0), 0)),
                  pl.BlockSpec((None, 8, CONV_DIM), lambda bi, i: (bi, 0, 0)),
                  pl.BlockSpec((CONV_W, CONV_DIM), lambda bi, i: (0, 0)),
                  pl.BlockSpec((1, W_C), lambda bi, i: (0, 0)),
                  pl.BlockSpec((1, W_C), lambda bi, i: (0, 0))],
        out_specs=[blk(W_C)] * 4,
        out_shape=[jax.ShapeDtypeStruct((b, t, W_C), I32), sds, sds, sds],
        compiler_params=_cp("parallel", "arbitrary"),
        name="gdn_prep",
    )(c_raw, c_raw, buf8, conv_w, a_log_l, dt_bias_l)


def _gdn_prompt_kernel(kq_ref, v_ref, eg_ref, be_ref, o_ref, s_ref, z_scr):
    i = pl.program_id(1)

    @pl.when(i == 0)
    def _():
        z_scr[...] = jnp.zeros(z_scr.shape, F32)

    ri = lax.broadcasted_iota(I32, (DK_C, LANES), 0)
    lj = lax.broadcasted_iota(I32, (DK_C, LANES), 1) % DV_C
    keep = lj >= ri

    def rot(r):
        a = pltpu.roll(jnp.broadcast_to(r, (DK_C, LANES)), 0, 1, stride=1, stride_axis=0)
        return jnp.where(keep, a, pltpu.roll(a, DV_C, 1))

    def step(t, carry):
        kqrow = kq_ref[pl.ds(t, 1), :]
        vrow = v_ref[pl.ds(t, 1), :]
        egrow = eg_ref[pl.ds(t, 1), :]
        berow = be_ref[pl.ds(t, 1), :]
        outs = []
        for p in range(H_C // 2):
            sl = slice(p * LANES, (p + 1) * LANES)
            word = rot(kqrow[:, sl])
            kk = pltpu.bitcast(word & jnp.int32(-65536), F32)
            qq = pltpu.bitcast(lax.shift_left(word, 16), F32)
            z = z_scr[p]
            ks = jnp.sum(kk * z, axis=0, keepdims=True)
            u = berow[:, sl] * (vrow[:, sl] - egrow[:, sl] * ks)
            z = egrow[:, sl] * z + kk * u
            z_scr[p] = z
            outs.append(jnp.sum(qq * z, axis=0, keepdims=True))
        o_ref[pl.ds(t, 1), :] = jnp.concatenate(outs, axis=1)
        return carry

    lax.fori_loop(0, kq_ref.shape[0], step, 0, unroll=4)

    @pl.when(i == pl.num_programs(1) - 1)
    def _():
        s_ref[...] = z_scr[...]


def _unrotate(z):
    n = z.shape[-1]
    d = jnp.arange(n)[:, None]
    v = jnp.arange(n)[None, :]
    idx = jnp.broadcast_to((v - d) % n, z.shape)
    return jnp.take_along_axis(z, idx, axis=-2)


def gdn_prompt(kq, v, eg, be):
    b, t, _ = v.shape
    blk = pl.BlockSpec((None, TB, W_C), lambda bi, i: (bi, i, 0))
    o, z = pl.pallas_call(
        _gdn_prompt_kernel,
        grid=(b, t // TB),
        in_specs=[blk] * 4,
        out_specs=[blk, pl.BlockSpec((None, H_C // 2, DK_C, LANES), lambda bi, i: (bi, 0, 0, 0))],
        out_shape=[jax.ShapeDtypeStruct((b, t, W_C), F32),
                   jax.ShapeDtypeStruct((b, H_C // 2, DK_C, LANES), F32)],
        scratch_shapes=[pltpu.VMEM((H_C // 2, DK_C, LANES), F32)],
        compiler_params=_cp("parallel", "arbitrary"),
        name="gdn_prompt",
    )(kq, v, eg, be)
    z = z.reshape(b, H_C // 2, DK_C, 2, DV_C).transpose(0, 1, 3, 2, 4).reshape(b, H_C, DK_C, DV_C)
    return o, _unrotate(z)


def _gdn_sample_kernel(qk_ref, vv_ref, wqk_ref, wv_ref, bc_ref, ac_ref, al_ref, dt_ref, s_ref, o_ref, sn_ref):
    yqk = jnp.zeros(qk_ref.shape[2:], F32)
    yv = jnp.zeros(vv_ref.shape[2:], F32)
    for tap in range(CONV_W):
        yqk = yqk + qk_ref[0, tap] * wqk_ref[tap]
        yv = yv + vv_ref[0, tap] * wv_ref[tap]
    yqk = _silu(yqk)
    vrow = _silu(yv)
    yq, yk = yqk[0:H_C], yqk[H_C:2 * H_C]
    qcol = yq * lax.rsqrt(jnp.sum(yq * yq, axis=1, keepdims=True) + EPS) * (DK_C ** -0.5)
    kcol = yk * lax.rsqrt(jnp.sum(yk * yk, axis=1, keepdims=True) + EPS)
    beta = jax.nn.sigmoid(bc_ref[0])
    x = ac_ref[0] + dt_ref[...]
    eg = jnp.exp(-jnp.exp(al_ref[...]) * (jnp.maximum(x, 0.0) + jnp.log1p(jnp.exp(-jnp.abs(x)))))
    s = s_ref[0]
    ks = jnp.sum(kcol * s, axis=1, keepdims=True)
    u = beta * (vrow - eg * ks)
    s_new = eg * s + kcol * u
    sn_ref[0] = s_new
    o_ref[0] = jnp.sum(qcol * s_new, axis=1, keepdims=True)


def gdn_sample(c_raw, conv_buf, conv_w, a_log_l, dt_bias_l, state):
    b = c_raw.shape[0]
    taps = jnp.concatenate([conv_buf, c_raw[:, None, 0:CONV_DIM]], axis=1)
    qk = taps[:, :, 0:2 * W_C].reshape(b, CONV_W, 2 * H_C, DK_C, 1)
    vv = taps[:, :, 2 * W_C:].reshape(b, CONV_W, H_C, 1, DV_C)
    wqk = conv_w[:, 0:2 * W_C].reshape(CONV_W, 2 * H_C, DK_C, 1)
    wv = conv_w[:, 2 * W_C:].reshape(CONV_W, H_C, 1, DV_C)
    bc = c_raw[:, CONV_DIM + W_C:CONV_DIM + W_C + H_C].reshape(b, H_C, 1, 1)
    ac = c_raw[:, CONV_DIM + W_C + H_C:CONV_DIM + W_C + 2 * H_C].reshape(b, H_C, 1, 1)
    per_seq = lambda shp: pl.BlockSpec((1,) + shp, lambda i: (i,) + (0,) * len(shp))
    const = lambda shp: pl.BlockSpec(shp, lambda i: (0,) * len(shp))
    o, s_new = pl.pallas_call(
        _gdn_sample_kernel,
        grid=(b,),
        in_specs=[per_seq((CONV_W, 2 * H_C, DK_C, 1)), per_seq((CONV_W, H_C, 1, DV_C)),
                  const((CONV_W, 2 * H_C, DK_C, 1)), const((CONV_W, H_C, 1, DV_C)),
                  per_seq((H_C, 1, 1)), per_seq((H_C, 1, 1)), const((H_C, 1, 1)), const((H_C, 1, 1)),
                  per_seq((H_C, DK_C, DV_C))],
        out_specs=[per_seq((H_C, 1, DV_C)), per_seq((H_C, DK_C, DV_C))],
        out_shape=[jax.ShapeDtypeStruct((b, H_C, 1, DV_C), F32),
                   jax.ShapeDtypeStruct((b, H_C, DK_C, DV_C), F32)],
        compiler_params=_cp("parallel"),
        name="gdn_sample",
    )(qk, vv, wqk, wv, bc, ac, a_log_l.reshape(H_C, 1, 1), dt_bias_l.reshape(H_C, 1, 1), state)
    return o.reshape(b, W_C), s_new, taps[:, 1:, :]


def _merge_kernel(x_ref, g_ref, oa_ref, ob_ref, oc_ref, gb_ref, zc_ref, gh_ref, gg_ref,
                  wa_ref, wb_ref, wc_ref, wo_ref, o_ref):
    ob = ob_ref[...]
    parts = []
    for h in range(H_B):
        seg = ob[:, h * DV_B:(h + 1) * DV_B]
        parts.append(seg * lax.rsqrt(jnp.mean(seg * seg, axis=-1, keepdims=True) + EPS))
    obn = jnp.concatenate(parts, axis=1) * gh_ref[...] * _silu(gb_ref[...])
    oc = oc_ref[...]
    ms = _dot_exact(oc * oc, _head_sum_matrix()) * (1.0 / DV_C)
    ocn = oc * lax.rsqrt(ms + EPS) * gg_ref[...] * _silu(zc_ref[...])
    m = (jax.nn.sigmoid(g_ref[:, 0:D_MODEL]) * _dot(oa_ref[...].astype(BF16), wa_ref[...])
         + jax.nn.sigmoid(g_ref[:, D_MODEL:2 * D_MODEL]) * _dot(obn.astype(BF16), wb_ref[...])
         + jax.nn.sigmoid(g_ref[:, 2 * D_MODEL:3 * D_MODEL]) * _dot(ocn.astype(BF16), wc_ref[...]))
    o_ref[...] = x_ref[...] + _dot(m.astype(BF16), wo_ref[...])


def merge(x, gates, oa, ob, oc, b_raw, c_raw, g_hgrn, g_gdn, wa, wb, wc, wo, tm):
    n = x.shape[0]
    row = lambda w: pl.BlockSpec((tm, w), lambda i: (i, 0))
    const = lambda r, c: pl.BlockSpec((r, c), lambda i: (0, 0))
    return pl.pallas_call(
        _merge_kernel,
        grid=(n // tm,),
        in_specs=[row(D_MODEL), row(WG_G), row(W_A), row(W_B), row(W_C),
                  pl.BlockSpec((tm, W_B), lambda i: (i, 3)),
                  pl.BlockSpec((tm, W_C), lambda i: (i, 3)),
                  const(1, W_B), const(1, W_C),
                  const(W_A, D_MODEL), const(W_B, D_MODEL), const(W_C, D_MODEL), const(D_MODEL, D_MODEL)],
        out_specs=row(D_MODEL),
        out_shape=jax.ShapeDtypeStruct((n, D_MODEL), F32),
        compiler_params=_cp("parallel"),
        name="merge",
    )(x, gates, oa, ob, oc, b_raw, c_raw, g_hgrn.reshape(1, W_B), g_gdn.reshape(1, W_C), wa, wb, wc, wo)


def _cross_prompt_kernel(x_ref, g_ref, wq_ref, mk_ref, mv_ref, wo_ref, o_ref):
    x = x_ref[...]
    q = _dot(_rms_rows(x, g_ref[...]).astype(BF16), wq_ref[...])
    outs = []
    for h in range(H_X):
        sl = slice(h * HD_X, (h + 1) * HD_X)
        lg = _dot_nt(q[:, sl].astype(BF16), mk_ref[:, sl]) * (HD_X ** -0.5)
        pr = jnp.exp(lg - jnp.max(lg, axis=-1, keepdims=True))
        pr = pr / jnp.sum(pr, axis=-1, keepdims=True)
        outs.append(_dot(pr.astype(BF16), mv_ref[:, sl]))
    o_ref[...] = x + _dot(jnp.concatenate(outs, axis=1).astype(BF16), wo_ref[...])


def cross_prompt(x, g, wq, mk, mv, wo, tm):
    b, t, d = x.shape
    const = lambda r, c: pl.BlockSpec((r, c), lambda bi, i: (0, 0))
    mem = pl.BlockSpec((None, N_MEM, W_X), lambda bi, i: (bi, 0, 0))
    return pl.pallas_call(
        _cross_prompt_kernel,
        grid=(b, t // tm),
        in_specs=[pl.BlockSpec((None, tm, d), lambda bi, i: (bi, i, 0)), const(1, d), const(d, W_X),
                  mem, mem, const(W_X, d)],
        out_specs=pl.BlockSpec((None, tm, d), lambda bi, i: (bi, i, 0)),
        out_shape=jax.ShapeDtypeStruct((b, t, d), F32),
        compiler_params=_cp("parallel", "parallel"),
        name="cross_prompt",
    )(x, g.reshape(1, d), wq, mk, mv, wo)


def _cross_sample_kernel(q_ref, mk_ref, mv_ref, o_ref):
    q = q_ref[0].astype(BF16).astype(F32)
    mk = mk_ref[0].astype(BF16).astype(F32)
    mv = mv_ref[0].astype(BF16).astype(F32)
    outs = []
    for h in range(H_X):
        sl = slice(h * HD_X, (h + 1) * HD_X)
        lg = jnp.sum(mk[:, sl] * q[:, sl], axis=1, keepdims=True) * (HD_X ** -0.5)
        pr = jnp.exp(lg - jnp.max(lg, axis=0, keepdims=True))
        pr = (pr / jnp.sum(pr, axis=0, keepdims=True)).astype(BF16).astype(F32)
        outs.append(jnp.sum(pr * mv[:, sl], axis=0, keepdims=True))
    o_ref[0] = jnp.concatenate(outs, axis=1)


def cross_sample(q, mk, mv):
    b = q.shape[0]
    mem = pl.BlockSpec((1, N_MEM, W_X), lambda i: (i, 0, 0))
    row = pl.BlockSpec((1, 1, W_X), lambda i: (i, 0, 0))
    return pl.pallas_call(
        _cross_sample_kernel,
        grid=(b,),
        in_specs=[row, mem, mem],
        out_specs=row,
        out_shape=jax.ShapeDtypeStruct((b, 1, W_X), F32),
        compiler_params=_cp("parallel"),
        name="cross_sample",
    )(q.reshape(b, 1, W_X), mk, mv).reshape(b, W_X)


def _group_weights(w_in):
    pad = lambda a, w: jnp.pad(a, ((0, 0), (0, 0), (0, w - a.shape[-1]))).astype(BF16)
    wa = pad(w_in[:, :, OFFS[0]:OFFS[6]], WG_A)
    wb = w_in[:, :, OFFS[6]:OFFS[10]].astype(BF16)
    wc = pad(w_in[:, :, OFFS[10]:OFFS[14]], WG_C)
    wg = w_in[:, :, OFFS[14]:OFFS[15]].astype(BF16)
    return wa, wb, wc, wg


def _rope_tables(pos):
    half = HD_A // 2
    inv = ROPE_THETA ** (-jnp.arange(half, dtype=F32) / half)
    ang = pos.astype(F32)[:, None] * inv[None, :]
    cos = jnp.tile(jnp.concatenate([jnp.cos(ang), jnp.cos(ang)], axis=1), (1, H_A))
    sin = jnp.tile(jnp.concatenate([-jnp.sin(ang), jnp.sin(ang)], axis=1), (1, H_A))
    return cos, sin


def _lower_bounds(lb_param):
    p = jax.nn.softmax(lb_param.astype(F32), axis=0)
    c = jnp.cumsum(p, axis=0)
    return c - c[0:1]


def kernel(x_prompt, x_sample, cache_attn_k, cache_attn_v, cache_idx_k, cache_mem_k, cache_mem_v, state_hgrn, state_gdn, state_conv, page_table, mem_prompt, norm_mix, w_in, g_kidx, lb_param, g_hgrn, conv_w, a_log, dt_bias, g_gdn, w_br_a, w_br_b, w_br_c, w_out, norm_cross, norm_mem, w_xq, w_xk, w_xv, w_xo, norm_mlp, w_up, w_down, norm_final):
    bp, tp, d = x_prompt.shape
    bs = x_sample.shape[0]
    n_pool = cache_attn_k.shape[1]
    past = page_table.shape[1] * PAGE
    npr = bp * tp
    tm_p = 512

    wa, wb, wc, wg = _group_weights(w_in)
    bf = lambda a: a.astype(BF16)
    w_br_a, w_br_b, w_br_c, w_out = bf(w_br_a), bf(w_br_b), bf(w_br_c), bf(w_out)
    w_xq, w_xk, w_xv, w_xo, w_up, w_down = bf(w_xq), bf(w_xk), bf(w_xv), bf(w_xo), bf(w_up), bf(w_down)
    lb = _lower_bounds(lb_param)
    gk = jnp.pad(g_kidx, ((0, 0), (0, LANES - D_IDX)))
    a_log_l = jnp.repeat(a_log, DK_C, axis=1)
    dt_bias_l = jnp.repeat(dt_bias, DK_C, axis=1)
    cos_p, sin_p = _rope_tables(jnp.arange(tp))
    cos_s, sin_s = _rope_tables(jnp.full((bs,), past))
    mem_flat = mem_prompt.reshape(bp * N_MEM, d)
    zero_buf = jnp.zeros((bp, 8, CONV_DIM), F32)

    xp = x_prompt.reshape(npr, d)
    xs = x_sample.reshape(bs, d)
    outs = {k: [] for k in ("kp", "vp", "ip", "mkp", "mvp", "hp", "gp", "cp", "ks", "vs", "is", "hs", "gs", "cs")}
    for l in range(DEPTH):
        a_raw = norm_linear(xp, norm_mix[l], wa[l], tm_p)
        b_raw = norm_linear(xp, norm_mix[l], wb[l], tm_p)
        c_raw = norm_linear(xp, norm_mix[l], wc[l], tm_p)
        gates = norm_linear(xp, norm_mix[l], wg[l], tm_p)
        q, k, kb, qi, tail, ki2 = dsa_prep(a_raw, cos_p, sin_p, gk[l:l + 1], tm_p, tp // tm_p)
        vt = v_chunks_t(a_raw).reshape(bp, tp // SC, W_A, SC)
        r3 = lambda a: a.reshape(bp, tp, a.shape[-1])
        oa = dsa_prompt(r3(q), r3(qi), r3(tail), r3(ki2), r3(kb), vt).reshape(npr, W_A)
        ob, sh = hgrn_prompt(r3(b_raw), lb[l])
        kq, vc, eg, be = gdn_prep(r3(c_raw), zero_buf, conv_w[l], a_log_l[l:l + 1], dt_bias_l[l:l + 1], tm_p)
        oc, sg = gdn_prompt(kq, vc, eg, be)
        xp = merge(xp, gates, oa, ob.reshape(npr, W_B), oc.reshape(npr, W_C), b_raw, c_raw,
                   g_hgrn[l], g_gdn[l], w_br_a[l], w_br_b[l], w_br_c[l], w_out[l], tm_p)
        mk = norm_linear(mem_flat, norm_mem[l], w_xk[l], N_MEM)
        mv = norm_linear(mem_flat, norm_mem[l], w_xv[l], N_MEM)
        xp = cross_prompt(xp.reshape(bp, tp, d), norm_cross[l], w_xq[l], bf(mk).reshape(bp, N_MEM, W_X),
                          bf(mv).reshape(bp, N_MEM, W_X), w_xo[l], tm_p).reshape(npr, d)
        xp = mlp(xp, norm_mlp[l], w_up[l], w_down[l], 1024, 1024)
        outs["kp"].append(k.reshape(bp, tp, H_A, HD_A))
        outs["vp"].append(a_raw[:, 2 * W_A:3 * W_A].reshape(bp, tp, H_A, HD_A))
        outs["ip"].append(tail[:, 0:D_IDX].reshape(bp, tp, D_IDX))
        outs["mkp"].append(mk.reshape(bp, N_MEM, H_X, HD_X))
        outs["mvp"].append(mv.reshape(bp, N_MEM, H_X, HD_X))
        outs["hp"].append(sh)
        outs["gp"].append(sg)
        outs["cp"].append(r3(c_raw)[:, tp - (CONV_W - 1):, 0:CONV_DIM])

        a_raw = norm_linear(xs, norm_mix[l], wa[l], bs)
        b_raw = norm_linear(xs, norm_mix[l], wb[l], bs)
        c_raw = norm_linear(xs, norm_mix[l], wc[l], bs)
        gates = norm_linear(xs, norm_mix[l], wg[l], bs)
        q, k, kb, qi, tail, ki2 = dsa_prep(a_raw, cos_s, sin_s, gk[l:l + 1], bs, 1)
        va = a_raw[:, 2 * W_A:3 * W_A]
        sc = dsa_sample_scores(page_table, l, qi, tail, cache_idx_k).reshape(bs, past)
        bias, new_sel = dsa_sample_select(sc, qi, tail)
        oa = dsa_sample_attend(page_table, l, q, k, va, bias, new_sel,
                               cache_attn_k, cache_attn_v).reshape(bs, W_A)
        ob, sh = hgrn_sample(b_raw, lb[l], state_hgrn[l])
        oc, sg, conv_new = gdn_sample(c_raw, state_conv[l], conv_w[l], a_log[l], dt_bias[l], state_gdn[l])
        xs = merge(xs, gates, oa, ob, oc, b_raw, c_raw,
                   g_hgrn[l], g_gdn[l], w_br_a[l], w_br_b[l], w_br_c[l], w_out[l], bs)
        qx = norm_linear(xs, norm_cross[l], w_xq[l], bs)
        ox = cross_sample(qx, cache_mem_k[l].reshape(bs, N_MEM, W_X), cache_mem_v[l].reshape(bs, N_MEM, W_X))
        xs = linear_residual(xs, ox, w_xo[l], bs)
        xs = mlp(xs, norm_mlp[l], w_up[l], w_down[l], bs, 1024)
        outs["ks"].append(k.reshape(bs, 1, H_A, HD_A))
        outs["vs"].append(va.reshape(bs, 1, H_A, HD_A))
        outs["is"].append(tail[:, 0:D_IDX].reshape(bs, 1, D_IDX))
        outs["hs"].append(sh)
        outs["gs"].append(sg)
        outs["cs"].append(conv_new)

    y_prompt = final_norm(xp, norm_final, tm_p).reshape(bp, tp, d)
    y_sample = final_norm(xs, norm_final, bs).reshape(bs, 1, d)
    st = jnp.stack
    return (y_prompt, y_sample,
            st(outs["kp"]), st(outs["vp"]), st(outs["ip"]), st(outs["mkp"]), st(outs["mvp"]),
            st(outs["hp"]), st(outs["gp"]), st(outs["cp"]),
            st(outs["ks"]), st(outs["vs"]), st(outs["is"]), st(outs["hs"]), st(outs["gs"]), st(outs["cs"]))
```

```python
import functools
import math

import jax
import jax.numpy as jnp
import numpy as np
from jax import lax
from jax.experimental import pallas as pl
from jax.experimental.pallas import tpu as pltpu

F32 = jnp.float32
BF16 = jnp.bfloat16
I32 = jnp.int32

D_MODEL = 1024
DEPTH = 4
PAGE = 128
H_A, HD_A = 8, 64
H_IDX, D_IDX = 8, 64
TOPK = 256
ROPE_THETA = 10000.0
H_B, DK_B, DV_B = 4, 128, 128
H_C, DK_C, DV_C = 8, 64, 64
CONV_W = 4
N_MEM = 256
H_X, HD_X = 4, 128
D_FF = 4 * D_MODEL
EPS = 1e-6

W_A = H_A * HD_A
W_B = H_B * DK_B
W_C = H_C * DK_C
CONV_DIM = 3 * W_C
W_X = H_X * HD_X
SPLITS = (W_A, W_A, W_A, W_A, D_IDX, H_IDX, W_B, W_B, W_B, W_B, CONV_DIM, W_C, H_C, H_C, 3 * D_MODEL)
OFFS = tuple(int(o) for o in np.cumsum((0,) + SPLITS))
WG_A = 2176
WG_B = 2048
WG_C = 2176
WG_G = 3 * D_MODEL

LANES = 128
SUBLANES = 8
VMEM_LIMIT = 56 * 1024 * 1024
NEG_INF = float("-inf")
KEY_NEG_INF = -2139095041
HIGHEST = lax.Precision.HIGHEST

TQ = 128
SC = 256
SUB = 16
TB = 256


def _cp(*sem):
    return pltpu.CompilerParams(dimension_semantics=sem, vmem_limit_bytes=VMEM_LIMIT)


def _rms_rows(x, g):
    return x * lax.rsqrt(jnp.mean(x * x, axis=-1, keepdims=True) + EPS) * g


def _silu(x):
    return x * jax.nn.sigmoid(x)


def _dot(a, b):
    return jnp.dot(a, b, preferred_element_type=F32)


def _dot_nt(a, b):
    return lax.dot_general(a, b, (((1,), (1,)), ((), ())), preferred_element_type=F32)


def _dot_exact(a, b):
    return jnp.dot(a, b, preferred_element_type=F32, precision=HIGHEST)


def _norm_linear_kernel(x_ref, g_ref, w_ref, o_ref):
    h = _rms_rows(x_ref[...], g_ref[...])
    o_ref[...] = _dot(h.astype(BF16), w_ref[...])


def norm_linear(x, g, w, tm):
    n, d = x.shape
    wd = w.shape[1]
    return pl.pallas_call(
        _norm_linear_kernel,
        grid=(n // tm,),
        in_specs=[pl.BlockSpec((tm, d), lambda i: (i, 0)),
                  pl.BlockSpec((1, d), lambda i: (0, 0)),
                  pl.BlockSpec((d, wd), lambda i: (0, 0))],
        out_specs=pl.BlockSpec((tm, wd), lambda i: (i, 0)),
        out_shape=jax.ShapeDtypeStruct((n, wd), F32),
        compiler_params=_cp("parallel"),
        name="norm_linear",
    )(x, g.reshape(1, d), w)


def _linear_residual_kernel(x_ref, a_ref, w_ref, o_ref):
    o_ref[...] = x_ref[...] + _dot(a_ref[...].astype(BF16), w_ref[...])


def linear_residual(x, a, w, tm):
    n, d = x.shape
    k = a.shape[1]
    return pl.pallas_call(
        _linear_residual_kernel,
        grid=(n // tm,),
        in_specs=[pl.BlockSpec((tm, d), lambda i: (i, 0)),
                  pl.BlockSpec((tm, k), lambda i: (i, 0)),
                  pl.BlockSpec((k, d), lambda i: (0, 0))],
        out_specs=pl.BlockSpec((tm, d), lambda i: (i, 0)),
        out_shape=jax.ShapeDtypeStruct((n, d), F32),
        compiler_params=_cp("parallel"),
        name="linear_residual",
    )(x, a, w)


def _mlp_kernel(x_ref, g_ref, wu_ref, wd_ref, o_ref, h_scr):
    j = pl.program_id(1)

    @pl.when(j == 0)
    def _():
        x = x_ref[...]
        h_scr[...] = _rms_rows(x, g_ref[...]).astype(BF16)
        o_ref[...] = x

    u = _dot(h_scr[...], wu_ref[...])
    a = jnp.square(jnp.maximum(u, 0.0)).astype(BF16)
    o_ref[...] += _dot(a, wd_ref[...])


def mlp(x, g, w_up, w_down, tm, tf):
    n, d = x.shape
    ff = w_up.shape[1]
    return pl.pallas_call(
        _mlp_kernel,
        grid=(n // tm, ff // tf),
        in_specs=[pl.BlockSpec((tm, d), lambda i, j: (i, 0)),
                  pl.BlockSpec((1, d), lambda i, j: (0, 0)),
                  pl.BlockSpec((d, tf), lambda i, j: (0, j)),
                  pl.BlockSpec((tf, d), lambda i, j: (j, 0))],
        out_specs=pl.BlockSpec((tm, d), lambda i, j: (i, 0)),
        out_shape=jax.ShapeDtypeStruct((n, d), F32),
        scratch_shapes=[pltpu.VMEM((tm, d), BF16)],
        compiler_params=_cp("parallel", "arbitrary"),
        name="mlp",
    )(x, g.reshape(1, d), w_up, w_down)


def _final_norm_kernel(x_ref, g_ref, o_ref):
    o_ref[...] = _rms_rows(x_ref[...], g_ref[...])


def final_norm(x, g, tm):
    n, d = x.shape
    return pl.pallas_call(
        _final_norm_kernel,
        grid=(n // tm,),
        in_specs=[pl.BlockSpec((tm, d), lambda i: (i, 0)), pl.BlockSpec((1, d), lambda i: (0, 0))],
        out_specs=pl.BlockSpec((tm, d), lambda i: (i, 0)),
        out_shape=jax.ShapeDtypeStruct((n, d), F32),
        compiler_params=_cp("parallel"),
        name="final_norm",
    )(x, g.reshape(1, d))


def _rope(x, c, s):
    w = x.shape[-1]
    lane = lax.broadcasted_iota(I32, x.shape, 1)
    first = (lane % HD_A) < (HD_A // 2)
    partner = jnp.where(first, pltpu.roll(x, w - HD_A // 2, 1), pltpu.roll(x, HD_A // 2, 1))
    return x * c + partner * s


def _dsa_prep_kernel(a_ref, c_ref, s_ref, gk_ref, q_ref, k_ref, kb_ref, qi_ref, tail_ref, ki2_ref):
    c = c_ref[...]
    s = s_ref[...]
    qa = a_ref[:, 0:W_A]
    ka = a_ref[:, W_A:2 * W_A]
    qi = a_ref[:, 3 * W_A:4 * W_A]
    t = a_ref[:, 4 * W_A:4 * W_A + LANES]
    q_ref[...] = (_rope(qa, c, s) * (HD_A ** -0.5)).astype(BF16)
    kr = _rope(ka, c, s)
    k_ref[...] = kr
    kb_ref[...] = kr.astype(BF16)
    qi_ref[...] = _rope(qi, c, s).astype(BF16)
    lane = lax.broadcasted_iota(I32, t.shape, 1)
    is_k = lane < D_IDX
    ms = jnp.sum(jnp.where(is_k, t * t, 0.0), axis=-1, keepdims=True) * (1.0 / D_IDX)
    kin = t * lax.rsqrt(ms + EPS) * gk_ref[...]
    kir = _rope(kin, c[:, :LANES], s[:, :LANES])
    is_w = jnp.logical_and(lane >= D_IDX, lane < D_IDX + H_IDX)
    tail_ref[...] = kir + jnp.where(is_w, t * (H_IDX ** -0.5 * D_IDX ** -0.5), 0.0)
    ki2_ref[...] = (kir + pltpu.roll(kir, D_IDX, 1)).astype(BF16)


def dsa_prep(a_raw, cos, sin, gk, tm, n_tab_tiles):
    n = a_raw.shape[0]
    row = lambda w: pl.BlockSpec((tm, w), lambda i: (i, 0))
    tab = pl.BlockSpec((tm, W_A), lambda i: (i % n_tab_tiles, 0))
    sds = lambda w, dt: jax.ShapeDtypeStruct((n, w), dt)
    return pl.pallas_call(
        _dsa_prep_kernel,
        grid=(n // tm,),
        in_specs=[row(WG_A), tab, tab, pl.BlockSpec((1, LANES), lambda i: (0, 0))],
        out_specs=[row(W_A), row(W_A), row(W_A), row(W_A), row(LANES), row(LANES)],
        out_shape=[sds(W_A, BF16), sds(W_A, F32), sds(W_A, BF16), sds(W_A, BF16),
                   sds(LANES, F32), sds(LANES, BF16)],
        compiler_params=_cp("parallel"),
        name="dsa_prep",
    )(a_raw, cos, sin, gk)


def _v_chunks_kernel(a_ref, o_ref):
    o_ref[0] = a_ref[...].T.astype(BF16)


def v_chunks_t(a_raw):
    n = a_raw.shape[0]
    return pl.pallas_call(
        _v_chunks_kernel,
        grid=(n // SC,),
        in_specs=[pl.BlockSpec((SC, W_A), lambda i: (i, 2))],
        out_specs=pl.BlockSpec((1, W_A, SC), lambda i: (i, 0, 0)),
        out_shape=jax.ShapeDtypeStruct((n // SC, W_A, SC), BF16),
        compiler_params=_cp("parallel"),
        name="v_chunks_t",
    )(a_raw)


def _sort_key(x):
    i = pltpu.bitcast(x, I32)
    return i ^ (lax.shift_right_arithmetic(i, 31) & 0x7FFFFFFF)


def _kth_largest_key(count_ge, shape):
    def cond(s):
        return jnp.logical_and(s[0] < 32, s[3] == 0)

    def body(s):
        b, t, done, _ = s
        cand = t + lax.shift_left(jnp.int32(1), 31 - b)
        n = count_ge(cand)
        done = jnp.where(n == TOPK, 1, done)
        return b + 1, jnp.where(n >= TOPK, cand, t), done, jnp.min(done)

    init = (jnp.int32(0), jnp.full(shape, -2 ** 31, I32), jnp.zeros(shape, I32), jnp.int32(0))
    return lax.while_loop(cond, body, init)[1]


def _dsa_prompt_kernel(q_ref, qi_ref, tail_ref, ki2_ref, k_ref, vt_ref, o_ref,
                       qis_scr, qm_scr, key_scr, acc_scr):
    i = pl.program_id(1)
    t0 = i * TQ
    nchunk = (t0 + TQ + SC - 1) // SC
    lane = lax.broadcasted_iota(I32, (TQ, LANES), 1)
    half = lane // HD_A

    for h in range(H_A):
        p, hl = h // 2, h % 2
        sl = slice(p * LANES, (p + 1) * LANES)
        qis_scr[h * TQ:(h + 1) * TQ, :] = jnp.where(half == hl, qi_ref[:, sl].astype(F32), 0.0).astype(BF16)
        qm_scr[h * TQ:(h + 1) * TQ, :] = jnp.where(half == hl, q_ref[:, sl].astype(F32), 0.0).astype(BF16)
    w_t = tail_ref[...].T

    key_s = lax.broadcasted_iota(I32, (SC, TQ), 0)
    qry_t = t0 + lax.broadcasted_iota(I32, (SC, TQ), 1)

    def score_chunk(c, carry):
        base = pl.multiple_of(c * SC, SC)
        s_all = _dot_nt(ki2_ref[pl.ds(base, SC), :], qis_scr[...])
        acc = jnp.zeros((SC, TQ), F32)
        for h in range(H_IDX):
            acc = acc + w_t[D_IDX + h:D_IDX + h + 1, :] * jnp.maximum(s_all[:, h * TQ:(h + 1) * TQ], 0.0)
        sc = jnp.where(key_s + base <= qry_t, acc, NEG_INF)
        key_scr[c] = _sort_key(sc)
        return carry

    lax.fori_loop(0, nchunk, score_chunk, 0)

    def count(pred):
        def body(c, part):
            hit = jnp.where(pred(key_scr[c], c), 1, 0)
            return part + jnp.sum(hit.reshape(SC // SUBLANES, SUBLANES, TQ), axis=0)
        part = lax.fori_loop(0, nchunk, body, jnp.zeros((SUBLANES, TQ), I32))
        return jnp.sum(part, axis=0, keepdims=True)

    thr = _kth_largest_key(lambda cand: count(lambda kx, c: kx >= cand), (1, TQ))
    n_ge = count(lambda kx, c: kx >= thr)
    excess = jnp.logical_and(n_ge > TOPK, thr > KEY_NEG_INF)

    @pl.when(jnp.max(excess.astype(I32)) > 0)
    def _():
        need = TOPK - count(lambda kx, c: kx > thr)

        def body(b, y):
            cand = y + lax.shift_left(jnp.int32(1), 13 - b)
            f = count(lambda kx, c: jnp.logical_and(kx == thr, key_s + c * SC < cand))
            return jnp.where(f < need, cand, y)

        cut = lax.fori_loop(0, 14, body, jnp.zeros((1, TQ), I32)) + 1
        cut = jnp.where(excess, cut, jnp.int32(2 ** 30))

        def drop(c, carry):
            kx = key_scr[c]
            dropped = jnp.logical_and(kx == thr, key_s + c * SC >= cut)
            key_scr[c] = jnp.where(dropped, KEY_NEG_INF, kx)
            return carry

        lax.fori_loop(0, nchunk, drop, 0)

    def to_bias(c, carry):
        sel = jnp.logical_and(key_scr[c] >= thr, key_s + c * SC <= qry_t)
        key_scr[c] = pltpu.bitcast(jnp.where(sel, 0.0, NEG_INF), I32)
        return carry

    lax.fori_loop(0, nchunk, to_bias, 0)

    npair = H_A // 2
    acc_scr[...] = jnp.zeros(acc_scr.shape, F32)

    def attend_chunk(c, carry):
        base = pl.multiple_of(c * SC, SC)
        bias = pltpu.bitcast(key_scr[c], F32)
        bias2 = jnp.concatenate([bias, bias], axis=1)
        pairs = [slice(p * LANES, (p + 1) * LANES) for p in range(npair)]
        lgs = [_dot_nt(k_ref[pl.ds(base, SC), pairs[p]], qm_scr[2 * p * TQ:(2 * p + 2) * TQ, :])
               for p in range(npair)]
        new, prs, alphas = [], [], []
        for p in range(npair):
            m_old, l_old = carry[p]
            x = lgs[p] + bias2
            m_new = jnp.maximum(m_old, jnp.max(x, axis=0, keepdims=True))
            m_safe = jnp.where(m_new == NEG_INF, 0.0, m_new)
            pr = jnp.exp(x - m_safe)
            alpha = jnp.exp(m_old - m_safe)
            new.append((m_new, alpha * l_old + jnp.sum(pr, axis=0, keepdims=True)))
            prs.append(pr.astype(BF16))
            alphas.append(alpha)
        pvs = [_dot(vt_ref[c, pairs[p], :], prs[p]) for p in range(npair)]
        for p in range(npair):
            acc_scr[p] = alphas[p] * acc_scr[p] + pvs[p]
        return tuple(new)

    init = tuple((jnp.full((1, 2 * TQ), NEG_INF, F32), jnp.zeros((1, 2 * TQ), F32)) for _ in range(npair))
    fin = lax.fori_loop(0, nchunk, attend_chunk, init)
    for p in range(npair):
        l_fin = fin[p][1]
        acc = acc_scr[p]
        o_t = jnp.concatenate([acc[0:HD_A, 0:TQ] / l_fin[:, 0:TQ],
                               acc[HD_A:2 * HD_A, TQ:2 * TQ] / l_fin[:, TQ:2 * TQ]], axis=0)
        o_ref[:, p * LANES:(p + 1) * LANES] = o_t.T


def dsa_prompt(q, qi, tail, ki2, kb, vt):
    b, t, _ = q.shape
    blk = lambda w: pl.BlockSpec((None, TQ, w), lambda bi, i: (bi, i, 0))
    full = lambda w: pl.BlockSpec((None, t, w), lambda bi, i: (bi, 0, 0), pipeline_mode=pl.Buffered(1))
    return pl.pallas_call(
        _dsa_prompt_kernel,
        grid=(b, t // TQ),
        in_specs=[blk(W_A), blk(W_A), blk(LANES), full(LANES), full(W_A),
                  pl.BlockSpec((None, t // SC, W_A, SC), lambda bi, i: (bi, 0, 0, 0),
                               pipeline_mode=pl.Buffered(1))],
        out_specs=blk(W_A),
        out_shape=jax.ShapeDtypeStruct((b, t, W_A), F32),
        scratch_shapes=[pltpu.VMEM((H_IDX * TQ, LANES), BF16),
                        pltpu.VMEM((H_A * TQ, LANES), BF16),
                        pltpu.VMEM((t // SC, SC, TQ), I32),
                        pltpu.VMEM((H_A // 2, LANES, 2 * TQ), F32)],
        compiler_params=_cp("parallel", "arbitrary"),
        name="dsa_prompt",
    )(q, qi, tail, ki2, kb, vt)


PG = 8


def _dsa_sample_score_kernel(pt_ref, qi_ref, w_ref, *refs):
    page_refs, o_ref = refs[:PG], refs[PG]
    j = pl.program_id(1)
    qh = qi_ref[0]
    w_col = w_ref[0]
    for g in range(PG):
        kpt = page_refs[g][...].astype(BF16)
        s = jnp.maximum(_dot(qh, kpt), 0.0)
        o_ref[0, :, pl.ds(pl.multiple_of((j * PG + g) * PAGE, PAGE), PAGE)] = jnp.sum(
            w_col * s, axis=0, keepdims=True)


def dsa_sample_scores(page_table, layer, qi, tail, pool_kidx_t):
    b, n_pages = page_table.shape
    page_spec = lambda g: pl.BlockSpec((None, None, D_IDX, PAGE),
                                       lambda bi, j, pt: (layer, pt[bi, j * PG + g], 0, 0))
    return pl.pallas_call(
        _dsa_sample_score_kernel,
        grid_spec=pltpu.PrefetchScalarGridSpec(
            num_scalar_prefetch=1,
            grid=(b, n_pages // PG),
            in_specs=[pl.BlockSpec((1, H_IDX, D_IDX), lambda bi, j, pt: (bi, 0, 0)),
                      pl.BlockSpec((1, H_IDX, 1), lambda bi, j, pt: (bi, 0, 0))]
                     + [page_spec(g) for g in range(PG)],
            out_specs=pl.BlockSpec((1, 1, n_pages * PAGE), lambda bi, j, pt: (bi, 0, 0)),
        ),
        out_shape=jax.ShapeDtypeStruct((b, 1, n_pages * PAGE), F32),
        compiler_params=_cp("parallel", "arbitrary"),
        name="dsa_sample_scores",
    )(page_table, qi.reshape(b, H_IDX, D_IDX), tail[:, D_IDX:D_IDX + H_IDX].reshape(b, H_IDX, 1),
      *([pool_kidx_t] * PG))


def _dsa_sample_select_kernel(sc_ref, qi_ref, tail_ref, bias_ref, newsel_ref):
    sc = sc_ref[...]
    nb, past = sc.shape
    qi = qi_ref[...]
    ki = tail_ref[...]
    lane = lax.broadcasted_iota(I32, ki.shape, 1)
    kidx = jnp.where(lane < D_IDX, ki, 0.0)
    kidx = (kidx + pltpu.roll(kidx, D_IDX, 1)).astype(BF16).astype(F32)
    sc_new = jnp.zeros((nb, 1), F32)
    for h in range(H_IDX):
        p, hl = h // 2, h % 2
        qh = qi[:, p * LANES:(p + 1) * LANES].astype(F32)
        d = jnp.sum(jnp.where(lane // D_IDX == hl, qh * kidx, 0.0), axis=1, keepdims=True)
        w = jnp.sum(jnp.where(lane == D_IDX + h, ki, 0.0), axis=1, keepdims=True)
        sc_new = sc_new + w * jnp.maximum(d, 0.0)
    key = _sort_key(sc + 0.0)
    key_new = _sort_key(sc_new + 0.0)

    def count_ge(cand):
        return (jnp.sum(jnp.where(key >= cand, 1, 0), axis=1, keepdims=True)
                + jnp.where(key_new >= cand, 1, 0))

    thr = _kth_largest_key(count_ge, (nb, 1))
    n_gt = jnp.sum(jnp.where(key > thr, 1, 0), axis=1, keepdims=True) + jnp.where(key_new > thr, 1, 0)
    need = TOPK - n_gt
    eq = key == thr
    pos = lax.broadcasted_iota(I32, sc.shape, 1)

    def body(b, y):
        cand = y + lax.shift_left(jnp.int32(1), 13 - b)
        f = jnp.sum(jnp.where(jnp.logical_and(eq, pos < cand), 1, 0), axis=1, keepdims=True)
        return jnp.where(f < need, cand, y)

    cut = lax.fori_loop(0, 14, body, jnp.zeros((nb, 1), I32)) + 1
    sel = jnp.logical_or(key > thr, jnp.logical_and(eq, pos < cut))
    n_sel = jnp.sum(jnp.where(sel, 1, 0), axis=1, keepdims=True)
    bias_ref[...] = jnp.where(sel, 0.0, NEG_INF)
    new_sel = jnp.logical_or(key_new > thr, jnp.logical_and(key_new == thr, n_sel < TOPK))
    newsel_ref[...] = jnp.broadcast_to(jnp.where(new_sel, 0.0, NEG_INF), newsel_ref.shape)


def dsa_sample_select(sc, qi, tail):
    b, past = sc.shape
    return pl.pallas_call(
        _dsa_sample_select_kernel,
        out_shape=[jax.ShapeDtypeStruct((b, past), F32), jax.ShapeDtypeStruct((b, LANES), F32)],
        compiler_params=pltpu.CompilerParams(vmem_limit_bytes=VMEM_LIMIT),
        name="dsa_sample_select",
    )(sc, qi, tail)


def _dsa_sample_attn_kernel(pt_ref, q_ref, kn_ref, vn_ref, bias_ref, nsel_ref, *refs):
    k_refs, v_refs, o_ref = refs[:PG], refs[PG:2 * PG], refs[2 * PG]
    m_scr, l_scr, acc_scr = refs[2 * PG + 1:]
    j = pl.program_id(1)
    qb = jnp.broadcast_to(q_ref[0], (H_A, HD_A, PAGE))
    lane0 = lax.broadcasted_iota(I32, (H_A, 1, PAGE), 2) == 0

    @pl.when(j == 0)
    def _():
        lg = jnp.sum(q_ref[0] * kn_ref[0], axis=1, keepdims=True) + nsel_ref[0][:, 0:1]
        m_scr[...] = lg
        pr = jnp.where(jnp.logical_and(lane0, lg > NEG_INF), 1.0, 0.0)
        l_scr[...] = pr
        acc_scr[...] = pr * vn_ref[0]

    for g in range(PG):
        bias = bias_ref[0, :, pl.ds(pl.multiple_of((j * PG + g) * PAGE, PAGE), PAGE)]
        x = jnp.sum(qb * k_refs[g][...], axis=1, keepdims=True) + bias
        m_old = m_scr[...]
        m_new = jnp.maximum(m_old, jnp.max(x, axis=2, keepdims=True))
        m_safe = jnp.where(m_new == NEG_INF, 0.0, m_new)
        pr = jnp.exp(x - m_safe)
        alpha = jnp.exp(m_old - m_safe)
        l_scr[...] = alpha * l_scr[...] + pr
        acc_scr[...] = alpha * acc_scr[...] + pr * v_refs[g][...]
        m_scr[...] = m_new

    @pl.when(j == pl.num_programs(1) - 1)
    def _():
        o_ref[0] = (jnp.sum(acc_scr[...], axis=2, keepdims=True)
                    / jnp.sum(l_scr[...], axis=2, keepdims=True))


def dsa_sample_attend(page_table, layer, q, k_new, v_new, bias, new_sel, cache_kt, cache_vt):
    b, n_pages = page_table.shape
    page_spec = lambda g: pl.BlockSpec((None, None, H_A, HD_A, PAGE),
                                       lambda bi, j, pt: (layer, pt[bi, j * PG + g], 0, 0, 0))
    cols = pl.BlockSpec((1, H_A, HD_A, 1), lambda bi, j, pt: (bi, 0, 0, 0))
    row = lambda w: pl.BlockSpec((1, 1, w), lambda bi, j, pt: (bi, 0, 0))
    col = lambda a: a.astype(F32).reshape(b, H_A, HD_A, 1)
    return pl.pallas_call(
        _dsa_sample_attn_kernel,
        grid_spec=pltpu.PrefetchScalarGridSpec(
            num_scalar_prefetch=1,
            grid=(b, n_pages // PG),
            in_specs=[cols, cols, cols, row(n_pages * PAGE), row(LANES)]
                     + [page_spec(g) for g in range(PG)] * 2,
            out_specs=cols,
            scratch_shapes=[pltpu.VMEM((H_A, 1, 1), F32), pltpu.VMEM((H_A, 1, PAGE), F32),
                            pltpu.VMEM((H_A, HD_A, PAGE), F32)],
        ),
        out_shape=jax.ShapeDtypeStruct((b, H_A, HD_A, 1), F32),
        compiler_params=_cp("parallel", "arbitrary"),
        name="dsa_sample_attend",
    )(page_table, col(q), col(k_new), col(v_new),
      bias.reshape(b, 1, n_pages * PAGE), new_sel.reshape(b, 1, LANES),
      *([cache_kt] * PG), *([cache_vt] * PG))


def _block_tri(n, blk, strict_upper):
    r = lax.broadcasted_iota(I32, (n, n), 0)
    c = lax.broadcasted_iota(I32, (n, n), 1)
    same = (r // blk) == (c // blk)
    tri = (c > r) if strict_upper else (c <= r)
    return jnp.where(jnp.logical_and(same, tri), 1.0, 0.0).astype(F32)


def _split_dot(x, w):
    hi = x.astype(BF16)
    lo = (x - hi.astype(F32)).astype(BF16)
    return _dot(hi, w) + _dot(lo, w)


def _hgrn_prompt_kernel(b_ref, lb_ref, llb_ref, o_ref, s_ref, st_scr):
    i = pl.program_id(1)
    tm = b_ref.shape[0]
    nsub = tm // SUB

    @pl.when(i == 0)
    def _():
        st_scr[...] = jnp.zeros(st_scr.shape, F32)

    lb = lb_ref[...]
    fb = b_ref[:, W_B:2 * W_B]
    q = _silu(b_ref[:, 0:W_B])
    v = b_ref[:, 2 * W_B:3 * W_B]
    kk = (1.0 - lb) * jax.nn.sigmoid(-fb)
    la = llb_ref[0:1, :]
    lbb = llb_ref[1:2, :] + (jnp.minimum(fb, 0.0) - jnp.log1p(jnp.exp(-jnp.abs(fb))))
    logf = jnp.maximum(la, lbb) + jnp.log1p(jnp.exp(-jnp.abs(la - lbb)))

    g = _dot_exact(_block_tri(tm, SUB, False), logf)
    r = _dot_exact(_block_tri(tm, SUB, True), logf)
    qg = (q * jnp.exp(g)).astype(BF16)
    kg = kk * jnp.exp(r)
    eg_last = jnp.exp(g + r)

    rows = lax.broadcasted_iota(I32, (tm, W_B), 0) % SUB
    hr = lax.broadcasted_iota(I32, (W_B, W_B), 0) // DK_B
    hc = lax.broadcasted_iota(I32, (W_B, W_B), 1) // DK_B
    head_ones = jnp.where(hr == hc, 1.0, 0.0).astype(BF16)
    o = jnp.zeros((tm, W_B), F32)
    for j in range(SUB):
        if j == 0:
            w = q * kk
            vj = v
        else:
            dec = jnp.exp(jnp.where(rows >= j, g - pltpu.roll(g, j, 0), NEG_INF))
            w = q * pltpu.roll(kk, j, 0) * dec
            vj = pltpu.roll(v, j, 0)
        o = o + _split_dot(w, head_ones) * vj

    rowi = lax.broadcasted_iota(I32, (tm, DK_B), 0) // SUB
    for h in range(H_B):
        sl = slice(h * DK_B, (h + 1) * DK_B)
        vt = v[:, sl].T.astype(BF16)
        qg_h = qg[:, sl]
        kg_h = kg[:, sl]
        st = st_scr[h]
        oh = jnp.zeros((tm, DV_B), F32)
        for c in range(nsub):
            inc = rowi == c
            oh = oh + jnp.where(inc, _dot_nt(qg_h, st.astype(BF16)), 0.0)
            st = st * eg_last[c * SUB:c * SUB + 1, sl] + _dot(vt, jnp.where(inc, kg_h, 0.0).astype(BF16))
        st_scr[h] = st
        o_ref[:, sl] = o[:, sl] + oh

    @pl.when(i == pl.num_programs(1) - 1)
    def _():
        for h in range(H_B):
            s_ref[h] = st_scr[h].T


def hgrn_prompt(b_raw, lb):
    b, t, _ = b_raw.shape
    tm = 128
    llb = jnp.stack([jnp.log(lb), jnp.log1p(-lb)])
    return pl.pallas_call(
        _hgrn_prompt_kernel,
        grid=(b, t // tm),
        in_specs=[pl.BlockSpec((None, tm, WG_B), lambda bi, i: (bi, i, 0)),
                  pl.BlockSpec((1, W_B), lambda bi, i: (0, 0)),
                  pl.BlockSpec((2, W_B), lambda bi, i: (0, 0))],
        out_specs=[pl.BlockSpec((None, tm, W_B), lambda bi, i: (bi, i, 0)),
                   pl.BlockSpec((None, H_B, DK_B, DV_B), lambda bi, i: (bi, 0, 0, 0))],
        out_shape=[jax.ShapeDtypeStruct((b, t, W_B), F32),
                   jax.ShapeDtypeStruct((b, H_B, DK_B, DV_B), F32)],
        scratch_shapes=[pltpu.VMEM((H_B, DV_B, DK_B), F32)],
        compiler_params=_cp("parallel", "arbitrary"),
        name="hgrn_prompt",
    )(b_raw, lb.reshape(1, W_B), llb)


def _hgrn_sample_kernel(qb_ref, fb_ref, ib_ref, lb_ref, s_ref, o_ref, sn_ref):
    lb = lb_ref[...]
    fb = fb_ref[0]
    f = lb + (1.0 - lb) * jax.nn.sigmoid(fb)
    kk = (1.0 - lb) * jax.nn.sigmoid(-fb)
    s_new = f * s_ref[0] + kk * ib_ref[0]
    sn_ref[0] = s_new
    o_ref[0] = jnp.sum(_silu(qb_ref[0]) * s_new, axis=1, keepdims=True)


def hgrn_sample(b_raw, lb, state):
    b = b_raw.shape[0]
    col = lambda a: a.reshape(b, H_B, DK_B, 1)
    cspec = pl.BlockSpec((1, H_B, DK_B, 1), lambda i: (i, 0, 0, 0))
    rspec = pl.BlockSpec((1, H_B, 1, DV_B), lambda i: (i, 0, 0, 0))
    sspec = pl.BlockSpec((1, H_B, DK_B, DV_B), lambda i: (i, 0, 0, 0))
    o, s_new = pl.pallas_call(
        _hgrn_sample_kernel,
        grid=(b,),
        in_specs=[cspec, cspec, rspec, pl.BlockSpec((H_B, DK_B, 1), lambda i: (0, 0, 0)), sspec],
        out_specs=[rspec, sspec],
        out_shape=[jax.ShapeDtypeStruct((b, H_B, 1, DV_B), F32),
                   jax.ShapeDtypeStruct((b, H_B, DK_B, DV_B), F32)],
        compiler_params=_cp("parallel"),
        name="hgrn_sample",
    )(col(b_raw[:, 0:W_B]), col(b_raw[:, W_B:2 * W_B]), b_raw[:, 2 * W_B:3 * W_B].reshape(b, H_B, 1, DV_B),
      lb.reshape(H_B, DK_B, 1), state)
    return o.reshape(b, W_B), s_new


def _head_sum_matrix():
    r = lax.broadcasted_iota(I32, (W_C, W_C), 0) // DK_C
    c = lax.broadcasted_iota(I32, (W_C, W_C), 1) // DK_C
    return jnp.where(r == c, 1.0, 0.0).astype(F32)


def _gdn_prep_kernel(c_ref, halo_ref, buf_ref, w_ref, al_ref, dt_ref, kq_ref, v_ref, eg_ref, be_ref):
    i = pl.program_id(1)
    tm = c_ref.shape[0]
    prev = jnp.where(i == 0, buf_ref[...], halo_ref[:, 0:CONV_DIM])
    win = jnp.concatenate([prev, c_ref[:, 0:CONV_DIM]], axis=0)
    y = jnp.zeros((tm, CONV_DIM), F32)
    for tap in range(CONV_W):
        off = 8 - (CONV_W - 1) + tap
        y = y + win[off:off + tm, :] * w_ref[tap:tap + 1, :]
    y = _silu(y)
    ones = _head_sum_matrix()
    qc = y[:, 0:W_C]
    kc = y[:, W_C:2 * W_C]
    qn = qc * lax.rsqrt(_dot_exact(qc * qc, ones) + EPS) * (DK_C ** -0.5)
    kn = kc * lax.rsqrt(_dot_exact(kc * kc, ones) + EPS)
    kbits = pltpu.bitcast(kn.astype(BF16).astype(F32), I32) & jnp.int32(-65536)
    qbits = lax.shift_right_logical(pltpu.bitcast(qn.astype(BF16).astype(F32), I32), 16)
    kq_ref[...] = kbits | qbits
    v_ref[...] = y[:, 2 * W_C:3 * W_C]
    tail = c_ref[:, CONV_DIM + W_C:CONV_DIM + W_C + LANES]
    r = lax.broadcasted_iota(I32, (LANES, W_C), 0)
    cc = lax.broadcasted_iota(I32, (LANES, W_C), 1) // DK_C
    bcl = _dot_exact(tail, jnp.where(r == cc, 1.0, 0.0).astype(F32))
    acl = _dot_exact(tail, jnp.where(r == cc + H_C, 1.0, 0.0).astype(F32))
    be_ref[...] = jax.nn.sigmoid(bcl)
    x = acl + dt_ref[...]
    softplus = jnp.maximum(x, 0.0) + jnp.log1p(jnp.exp(-jnp.abs(x)))
    eg_ref[...] = jnp.exp(-jnp.exp(al_ref[...]) * softplus)


def gdn_prep(c_raw, buf8, conv_w, a_log_l, dt_bias_l, tm):
    b, t, _ = c_raw.shape
    blk = lambda w: pl.BlockSpec((None, tm, w), lambda bi, i: (bi, i, 0))
    sds = jax.ShapeDtypeStruct((b, t, W_C), F32)
    return pl.pallas_call(
        _gdn_prep_kernel,
        grid=(b, t // tm),
        in_specs=[blk(WG_C),
                  pl.BlockSpec((None, 8, WG_C), lambda bi, i: (bi, jnp.maximum(i * (tm // 8) - 1, 0), 0)),
                  pl.BlockSpec((None, 8, CONV_DIM), lambda bi, i: (bi, 0, 0)),
                  pl.BlockSpec((CONV_W, CONV_DIM), lambda bi, i: (0, 0)),
                  pl.BlockSpec((1, W_C), lambda bi, i: (0, 0)),
                  pl.BlockSpec((1, W_C), lambda bi, i: (0, 0))],
        out_specs=[blk(W_C)] * 4,
        out_shape=[jax.ShapeDtypeStruct((b, t, W_C), I32), sds, sds, sds],
        compiler_params=_cp("parallel", "arbitrary"),
        name="gdn_prep",
    )(c_raw, c_raw, buf8, conv_w, a_log_l, dt_bias_l)


def _gdn_prompt_kernel(kq_ref, v_ref, eg_ref, be_ref, o_ref, s_ref, z_scr):
    i = pl.program_id(1)

    @pl.when(i == 0)
    def _():
        z_scr[...] = jnp.zeros(z_scr.shape, F32)

    ri = lax.broadcasted_iota(I32, (DK_C, LANES), 0)
    lj = lax.broadcasted_iota(I32, (DK_C, LANES), 1) % DV_C
    keep = lj >= ri

    def rot(r):
        a = pltpu.roll(jnp.broadcast_to(r, (DK_C, LANES)), 0, 1, stride=1, stride_axis=0)
        return jnp.where(keep, a, pltpu.roll(a, DV_C, 1))

    def step(t, carry):
        kqrow = kq_ref[pl.ds(t, 1), :]
        vrow = v_ref[pl.ds(t, 1), :]
        egrow = eg_ref[pl.ds(t, 1), :]
        berow = be_ref[pl.ds(t, 1), :]
        outs = []
        for p in range(H_C // 2):
            sl = slice(p * LANES, (p + 1) * LANES)
            word = rot(kqrow[:, sl])
            kk = pltpu.bitcast(word & jnp.int32(-65536), F32)
            qq = pltpu.bitcast(lax.shift_left(word, 16), F32)
            z = z_scr[p]
            ks = jnp.sum(kk * z, axis=0, keepdims=True)
            u = berow[:, sl] * (vrow[:, sl] - egrow[:, sl] * ks)
            z = egrow[:, sl] * z + kk * u
            z_scr[p] = z
            outs.append(jnp.sum(qq * z, axis=0, keepdims=True))
        o_ref[pl.ds(t, 1), :] = jnp.concatenate(outs, axis=1)
        return carry

    lax.fori_loop(0, kq_ref.shape[0], step, 0, unroll=4)

    @pl.when(i == pl.num_programs(1) - 1)
    def _():
        s_ref[...] = z_scr[...]


def _unrotate(z):
    n = z.shape[-1]
    d = jnp.arange(n)[:, None]
    v = jnp.arange(n)[None, :]
    idx = jnp.broadcast_to((v - d) % n, z.shape)
    return jnp.take_along_axis(z, idx, axis=-2)


def gdn_prompt(kq, v, eg, be):
    b, t, _ = v.shape
    blk = pl.BlockSpec((None, TB, W_C), lambda bi, i: (bi, i, 0))
    o, z = pl.pallas_call(
        _gdn_prompt_kernel,
        grid=(b, t // TB),
        in_specs=[blk] * 4,
        out_specs=[blk, pl.BlockSpec((None, H_C // 2, DK_C, LANES), lambda bi, i: (bi, 0, 0, 0))],
        out_shape=[jax.ShapeDtypeStruct((b, t, W_C), F32),
                   jax.ShapeDtypeStruct((b, H_C // 2, DK_C, LANES), F32)],
        scratch_shapes=[pltpu.VMEM((H_C // 2, DK_C, LANES), F32)],
        compiler_params=_cp("parallel", "arbitrary"),
        name="gdn_prompt",
    )(kq, v, eg, be)
    z = z.reshape(b, H_C // 2, DK_C, 2, DV_C).transpose(0, 1, 3, 2, 4).reshape(b, H_C, DK_C, DV_C)
    return o, _unrotate(z)


def _gdn_sample_kernel(qk_ref, vv_ref, wqk_ref, wv_ref, bc_ref, ac_ref, al_ref, dt_ref, s_ref, o_ref, sn_ref):
    yqk = jnp.zeros(qk_ref.shape[2:], F32)
    yv = jnp.zeros(vv_ref.shape[2:], F32)
    for tap in range(CONV_W):
        yqk = yqk + qk_ref[0, tap] * wqk_ref[tap]
        yv = yv + vv_ref[0, tap] * wv_ref[tap]
    yqk = _silu(yqk)
    vrow = _silu(yv)
    yq, yk = yqk[0:H_C], yqk[H_C:2 * H_C]
    qcol = yq * lax.rsqrt(jnp.sum(yq * yq, axis=1, keepdims=True) + EPS) * (DK_C ** -0.5)
    kcol = yk * lax.rsqrt(jnp.sum(yk * yk, axis=1, keepdims=True) + EPS)
    beta = jax.nn.sigmoid(bc_ref[0])
    x = ac_ref[0] + dt_ref[...]
    eg = jnp.exp(-jnp.exp(al_ref[...]) * (jnp.maximum(x, 0.0) + jnp.log1p(jnp.exp(-jnp.abs(x)))))
    s = s_ref[0]
    ks = jnp.sum(kcol * s, axis=1, keepdims=True)
    u = beta * (vrow - eg * ks)
    s_new = eg * s + kcol * u
    sn_ref[0] = s_new
    o_ref[0] = jnp.sum(qcol * s_new, axis=1, keepdims=True)


def gdn_sample(c_raw, conv_buf, conv_w, a_log_l, dt_bias_l, state):
    b = c_raw.shape[0]
    taps = jnp.concatenate([conv_buf, c_raw[:, None, 0:CONV_DIM]], axis=1)
    qk = taps[:, :, 0:2 * W_C].reshape(b, CONV_W, 2 * H_C, DK_C, 1)
    vv = taps[:, :, 2 * W_C:].reshape(b, CONV_W, H_C, 1, DV_C)
    wqk = conv_w[:, 0:2 * W_C].reshape(CONV_W, 2 * H_C, DK_C, 1)
    wv = conv_w[:, 2 * W_C:].reshape(CONV_W, H_C, 1, DV_C)
    bc = c_raw[:, CONV_DIM + W_C:CONV_DIM + W_C + H_C].reshape(b, H_C, 1, 1)
    ac = c_raw[:, CONV_DIM + W_C + H_C:CONV_DIM + W_C + 2 * H_C].reshape(b, H_C, 1, 1)
    per_seq = lambda shp: pl.BlockSpec((1,) + shp, lambda i: (i,) + (0,) * len(shp))
    const = lambda shp: pl.BlockSpec(shp, lambda i: (0,) * len(shp))
    o, s_new = pl.pallas_call(
        _gdn_sample_kernel,
        grid=(b,),
        in_specs=[per_seq((CONV_W, 2 * H_C, DK_C, 1)), per_seq((CONV_W, H_C, 1, DV_C)),
                  const((CONV_W, 2 * H_C, DK_C, 1)), const((CONV_W, H_C, 1, DV_C)),
                  per_seq((H_C, 1, 1)), per_seq((H_C, 1, 1)), const((H_C, 1, 1)), const((H_C, 1, 1)),
                  per_seq((H_C, DK_C, DV_C))],
        out_specs=[per_seq((H_C, 1, DV_C)), per_seq((H_C, DK_C, DV_C))],
        out_shape=[jax.ShapeDtypeStruct((b, H_C, 1, DV_C), F32),
                   jax.ShapeDtypeStruct((b, H_C, DK_C, DV_C), F32)],
        compiler_params=_cp("parallel"),
        name="gdn_sample",
    )(qk, vv, wqk, wv, bc, ac, a_log_l.reshape(H_C, 1, 1), dt_bias_l.reshape(H_C, 1, 1), state)
    return o.reshape(b, W_C), s_new, taps[:, 1:, :]


def _merge_kernel(x_ref, g_ref, oa_ref, ob_ref, oc_ref, gb_ref, zc_ref, gh_ref, gg_ref,
                  wa_ref, wb_ref, wc_ref, wo_ref, o_ref):
    ob = ob_ref[...]
    parts = []
    for h in range(H_B):
        seg = ob[:, h * DV_B:(h + 1) * DV_B]
        parts.append(seg * lax.rsqrt(jnp.mean(seg * seg, axis=-1, keepdims=True) + EPS))
    obn = jnp.concatenate(parts, axis=1) * gh_ref[...] * _silu(gb_ref[...])
    oc = oc_ref[...]
    ms = _dot_exact(oc * oc, _head_sum_matrix()) * (1.0 / DV_C)
    ocn = oc * lax.rsqrt(ms + EPS) * gg_ref[...] * _silu(zc_ref[...])
    m = (jax.nn.sigmoid(g_ref[:, 0:D_MODEL]) * _dot(oa_ref[...].astype(BF16), wa_ref[...])
         + jax.nn.sigmoid(g_ref[:, D_MODEL:2 * D_MODEL]) * _dot(obn.astype(BF16), wb_ref[...])
         + jax.nn.sigmoid(g_ref[:, 2 * D_MODEL:3 * D_MODEL]) * _dot(ocn.astype(BF16), wc_ref[...]))
    o_ref[...] = x_ref[...] + _dot(m.astype(BF16), wo_ref[...])


def merge(x, gates, oa, ob, oc, b_raw, c_raw, g_hgrn, g_gdn, wa, wb, wc, wo, tm):
    n = x.shape[0]
    row = lambda w: pl.BlockSpec((tm, w), lambda i: (i, 0))
    const = lambda r, c: pl.BlockSpec((r, c), lambda i: (0, 0))
    return pl.pallas_call(
        _merge_kernel,
        grid=(n // tm,),
        in_specs=[row(D_MODEL), row(WG_G), row(W_A), row(W_B), row(W_C),
                  pl.BlockSpec((tm, W_B), lambda i: (i, 3)),
                  pl.BlockSpec((tm, W_C), lambda i: (i, 3)),
                  const(1, W_B), const(1, W_C),
                  const(W_A, D_MODEL), const(W_B, D_MODEL), const(W_C, D_MODEL), const(D_MODEL, D_MODEL)],
        out_specs=row(D_MODEL),
        out_shape=jax.ShapeDtypeStruct((n, D_MODEL), F32),
        compiler_params=_cp("parallel"),
        name="merge",
    )(x, gates, oa, ob, oc, b_raw, c_raw, g_hgrn.reshape(1, W_B), g_gdn.reshape(1, W_C), wa, wb, wc, wo)


def _cross_prompt_kernel(x_ref, g_ref, wq_ref, mk_ref, mv_ref, wo_ref, o_ref):
    x = x_ref[...]
    q = _dot(_rms_rows(x, g_ref[...]).astype(BF16), wq_ref[...])
    outs = []
    for h in range(H_X):
        sl = slice(h * HD_X, (h + 1) * HD_X)
        lg = _dot_nt(q[:, sl].astype(BF16), mk_ref[:, sl]) * (HD_X ** -0.5)
        pr = jnp.exp(lg - jnp.max(lg, axis=-1, keepdims=True))
        pr = pr / jnp.sum(pr, axis=-1, keepdims=True)
        outs.append(_dot(pr.astype(BF16), mv_ref[:, sl]))
    o_ref[...] = x + _dot(jnp.concatenate(outs, axis=1).astype(BF16), wo_ref[...])


def cross_prompt(x, g, wq, mk, mv, wo, tm):
    b, t, d = x.shape
    const = lambda r, c: pl.BlockSpec((r, c), lambda bi, i: (0, 0))
    mem = pl.BlockSpec((None, N_MEM, W_X), lambda bi, i: (bi, 0, 0))
    return pl.pallas_call(
        _cross_prompt_kernel,
        grid=(b, t // tm),
        in_specs=[pl.BlockSpec((None, tm, d), lambda bi, i: (bi, i, 0)), const(1, d), const(d, W_X),
                  mem, mem, const(W_X, d)],
        out_specs=pl.BlockSpec((None, tm, d), lambda bi, i: (bi, i, 0)),
        out_shape=jax.ShapeDtypeStruct((b, t, d), F32),
        compiler_params=_cp("parallel", "parallel"),
        name="cross_prompt",
    )(x, g.reshape(1, d), wq, mk, mv, wo)


def _cross_sample_kernel(q_ref, mk_ref, mv_ref, o_ref):
    q = q_ref[0].astype(BF16).astype(F32)
    mk = mk_ref[0].astype(BF16).astype(F32)
    mv = mv_ref[0].astype(BF16).astype(F32)
    outs = []
    for h in range(H_X):
        sl = slice(h * HD_X, (h + 1) * HD_X)
        lg = jnp.sum(mk[:, sl] * q[:, sl], axis=1, keepdims=True) * (HD_X ** -0.5)
        pr = jnp.exp(lg - jnp.max(lg, axis=0, keepdims=True))
        pr = (pr / jnp.sum(pr, axis=0, keepdims=True)).astype(BF16).astype(F32)
        outs.append(jnp.sum(pr * mv[:, sl], axis=0, keepdims=True))
    o_ref[0] = jnp.concatenate(outs, axis=1)


def cross_sample(q, mk, mv):
    b = q.shape[0]
    mem = pl.BlockSpec((1, N_MEM, W_X), lambda i: (i, 0, 0))
    row = pl.BlockSpec((1, 1, W_X), lambda i: (i, 0, 0))
    return pl.pallas_call(
        _cross_sample_kernel,
        grid=(b,),
        in_specs=[row, mem, mem],
        out_specs=row,
        out_shape=jax.ShapeDtypeStruct((b, 1, W_X), F32),
        compiler_params=_cp("parallel"),
        name="cross_sample",
    )(q.reshape(b, 1, W_X), mk, mv).reshape(b, W_X)


def _group_weights(w_in):
    pad = lambda a, w: jnp.pad(a, ((0, 0), (0, 0), (0, w - a.shape[-1]))).astype(BF16)
    wa = pad(w_in[:, :, OFFS[0]:OFFS[6]], WG_A)
    wb = w_in[:, :, OFFS[6]:OFFS[10]].astype(BF16)
    wc = pad(w_in[:, :, OFFS[10]:OFFS[14]], WG_C)
    wg = w_in[:, :, OFFS[14]:OFFS[15]].astype(BF16)
    return wa, wb, wc, wg


def _rope_tables(pos):
    half = HD_A // 2
    inv = ROPE_THETA ** (-jnp.arange(half, dtype=F32) / half)
    ang = pos.astype(F32)[:, None] * inv[None, :]
    cos = jnp.tile(jnp.concatenate([jnp.cos(ang), jnp.cos(ang)], axis=1), (1, H_A))
    sin = jnp.tile(jnp.concatenate([-jnp.sin(ang), jnp.sin(ang)], axis=1), (1, H_A))
    return cos, sin


def _lower_bounds(lb_param):
    p = jax.nn.softmax(lb_param.astype(F32), axis=0)
    c = jnp.cumsum(p, axis=0)
    return c - c[0:1]


def kernel(x_prompt, x_sample, cache_attn_k, cache_attn_v, cache_idx_k, cache_mem_k, cache_mem_v, state_hgrn, state_gdn, state_conv, page_table, mem_prompt, norm_mix, w_in, g_kidx, lb_param, g_hgrn, conv_w, a_log, dt_bias, g_gdn, w_br_a, w_br_b, w_br_c, w_out, norm_cross, norm_mem, w_xq, w_xk, w_xv, w_xo, norm_mlp, w_up, w_down, norm_final):
    bp, tp, d = x_prompt.shape
    bs = x_sample.shape[0]
    past = page_table.shape[1] * PAGE
    npr = bp * tp
    tm_p = 512

    wa, wb, wc, wg = _group_weights(w_in)
    bf = lambda a: a.astype(BF16)
    w_br_a, w_br_b, w_br_c, w_out = bf(w_br_a), bf(w_br_b), bf(w_br_c), bf(w_out)
    w_xq, w_xk, w_xv, w_xo, w_up, w_down = bf(w_xq), bf(w_xk), bf(w_xv), bf(w_xo), bf(w_up), bf(w_down)
    lb = _lower_bounds(lb_param)
    gk = jnp.pad(g_kidx, ((0, 0), (0, LANES - D_IDX)))
    a_log_l = jnp.repeat(a_log, DK_C, axis=1)
    dt_bias_l = jnp.repeat(dt_bias, DK_C, axis=1)
    cos_p, sin_p = _rope_tables(jnp.arange(tp))
    cos_s, sin_s = _rope_tables(jnp.full((bs,), past))
    cache_kt = jnp.transpose(cache_attn_k, (0, 1, 3, 4, 2))
    cache_vt = jnp.transpose(cache_attn_v, (0, 1, 3, 4, 2))
    cache_kidx_t = jnp.transpose(cache_idx_k, (0, 1, 3, 2))
    mem_flat = mem_prompt.reshape(bp * N_MEM, d)
    zero_buf = jnp.zeros((bp, 8, CONV_DIM), F32)

    xp = x_prompt.reshape(npr, d)
    xs = x_sample.reshape(bs, d)
    outs = {k: [] for k in ("kp", "vp", "ip", "mkp", "mvp", "hp", "gp", "cp", "ks", "vs", "is", "hs", "gs", "cs")}
    for l in range(DEPTH):
        a_raw = norm_linear(xp, norm_mix[l], wa[l], tm_p)
        b_raw = norm_linear(xp, norm_mix[l], wb[l], tm_p)
        c_raw = norm_linear(xp, norm_mix[l], wc[l], tm_p)
        gates = norm_linear(xp, norm_mix[l], wg[l], tm_p)
        q, k, kb, qi, tail, ki2 = dsa_prep(a_raw, cos_p, sin_p, gk[l:l + 1], tm_p, tp // tm_p)
        vt = v_chunks_t(a_raw).reshape(bp, tp // SC, W_A, SC)
        r3 = lambda a: a.reshape(bp, tp, a.shape[-1])
        oa = dsa_prompt(r3(q), r3(qi), r3(tail), r3(ki2), r3(kb), vt).reshape(npr, W_A)
        ob, sh = hgrn_prompt(r3(b_raw), lb[l])
        kq, vc, eg, be = gdn_prep(r3(c_raw), zero_buf, conv_w[l], a_log_l[l:l + 1], dt_bias_l[l:l + 1], tm_p)
        oc, sg = gdn_prompt(kq, vc, eg, be)
        xp = merge(xp, gates, oa, ob.reshape(npr, W_B), oc.reshape(npr, W_C), b_raw, c_raw,
                   g_hgrn[l], g_gdn[l], w_br_a[l], w_br_b[l], w_br_c[l], w_out[l], tm_p)
        mk = norm_linear(mem_flat, norm_mem[l], w_xk[l], N_MEM)
        mv = norm_linear(mem_flat, norm_mem[l], w_xv[l], N_MEM)
        xp = cross_prompt(xp.reshape(bp, tp, d), norm_cross[l], w_xq[l], bf(mk).reshape(bp, N_MEM, W_X),
                          bf(mv).reshape(bp, N_MEM, W_X), w_xo[l], tm_p).reshape(npr, d)
        xp = mlp(xp, norm_mlp[l], w_up[l], w_down[l], 1024, 1024)
        outs["kp"].append(k.reshape(bp, tp, H_A, HD_A))
        outs["vp"].append(a_raw[:, 2 * W_A:3 * W_A].reshape(bp, tp, H_A, HD_A))
        outs["ip"].append(tail[:, 0:D_IDX].reshape(bp, tp, D_IDX))
        outs["mkp"].append(mk.reshape(bp, N_MEM, H_X, HD_X))
        outs["mvp"].append(mv.reshape(bp, N_MEM, H_X, HD_X))
        outs["hp"].append(sh)
        outs["gp"].append(sg)
        outs["cp"].append(r3(c_raw)[:, tp - (CONV_W - 1):, 0:CONV_DIM])

        a_raw = norm_linear(xs, norm_mix[l], wa[l], bs)
        b_raw = norm_linear(xs, norm_mix[l], wb[l], bs)
        c_raw = norm_linear(xs, norm_mix[l], wc[l], bs)
        gates = norm_linear(xs, norm_mix[l], wg[l], bs)
        q, k, kb, qi, tail, ki2 = dsa_prep(a_raw, cos_s, sin_s, gk[l:l + 1], bs, 1)
        va = a_raw[:, 2 * W_A:3 * W_A]
        sc = dsa_sample_scores(page_table, l, qi, tail, cache_kidx_t).reshape(bs, past)
        bias, new_sel = dsa_sample_select(sc, qi, tail)
        oa = dsa_sample_attend(page_table, l, q, k, va, bias, new_sel, cache_kt, cache_vt).reshape(bs, W_A)
        ob, sh = hgrn_sample(b_raw, lb[l], state_hgrn[l])
        oc, sg, conv_new = gdn_sample(c_raw, state_conv[l], conv_w[l], a_log[l], dt_bias[l], state_gdn[l])
        xs = merge(xs, gates, oa, ob, oc, b_raw, c_raw,
                   g_hgrn[l], g_gdn[l], w_br_a[l], w_br_b[l], w_br_c[l], w_out[l], bs)
        qx = norm_linear(xs, norm_cross[l], w_xq[l], bs)
        ox = cross_sample(qx, cache_mem_k[l].reshape(bs, N_MEM, W_X), cache_mem_v[l].reshape(bs, N_MEM, W_X))
        xs = linear_residual(xs, ox, w_xo[l], bs)
        xs = mlp(xs, norm_mlp[l], w_up[l], w_down[l], bs, 1024)
        outs["ks"].append(k.reshape(bs, 1, H_A, HD_A))
        outs["vs"].append(va.reshape(bs, 1, H_A, HD_A))
        outs["is"].append(tail[:, 0:D_IDX].reshape(bs, 1, D_IDX))
        outs["hs"].append(sh)
        outs["gs"].append(sg)
        outs["cs"].append(conv_new)

    y_prompt = final_norm(xp, norm_final, tm_p).reshape(bp, tp, d)
    y_sample = final_norm(xs, norm_final, bs).reshape(bs, 1, d)
    st = jnp.stack
    return (y_prompt, y_sample,
            st(outs["kp"]), st(outs["vp"]), st(outs["ip"]), st(outs["mkp"]), st(outs["mvp"]),
            st(outs["hp"]), st(outs["gp"]), st(outs["cp"]),
            st(outs["ks"]), st(outs["vs"]), st(outs["is"]), st(outs["hs"]), st(outs["gs"]), st(outs["cs"]))
```

```python
import functools
import math

import jax
import jax.numpy as jnp
import numpy as np
from jax import lax
from jax.experimental import pallas as pl
from jax.experimental.pallas import tpu as pltpu

F32 = jnp.float32
BF16 = jnp.bfloat16
I32 = jnp.int32

D_MODEL = 1024
DEPTH = 4
PAGE = 128
H_A, HD_A = 8, 64
H_IDX, D_IDX = 8, 64
TOPK = 256
ROPE_THETA = 10000.0
H_B, DK_B, DV_B = 4, 128, 128
H_C, DK_C, DV_C = 8, 64, 64
CONV_W = 4
N_MEM = 256
H_X, HD_X = 4, 128
D_FF = 4 * D_MODEL
EPS = 1e-6

W_A = H_A * HD_A
W_B = H_B * DK_B
W_C = H_C * DK_C
CONV_DIM = 3 * W_C
W_X = H_X * HD_X
SPLITS = (W_A, W_A, W_A, W_A, D_IDX, H_IDX, W_B, W_B, W_B, W_B, CONV_DIM, W_C, H_C, H_C, 3 * D_MODEL)
OFFS = tuple(int(o) for o in np.cumsum((0,) + SPLITS))
WG_A = 2176
WG_B = 2048
WG_C = 2176
WG_G = 3 * D_MODEL

LANES = 128
SUBLANES = 8
VMEM_LIMIT = 56 * 1024 * 1024
NEG_INF = float("-inf")
KEY_NEG_INF = -2139095041
HIGHEST = lax.Precision.HIGHEST

TQ = 128
SC = 256
SUB = 16
TB = 256


def _cp(*sem):
    return pltpu.CompilerParams(dimension_semantics=sem, vmem_limit_bytes=VMEM_LIMIT)


def _rms_rows(x, g):
    return x * lax.rsqrt(jnp.mean(x * x, axis=-1, keepdims=True) + EPS) * g


def _silu(x):
    return x * jax.nn.sigmoid(x)


def _dot(a, b):
    return jnp.dot(a, b, preferred_element_type=F32)


def _dot_nt(a, b):
    return lax.dot_general(a, b, (((1,), (1,)), ((), ())), preferred_element_type=F32)


def _dot_exact(a, b):
    return jnp.dot(a, b, preferred_element_type=F32, precision=HIGHEST)


def _norm_linear_kernel(x_ref, g_ref, w_ref, o_ref):
    h = _rms_rows(x_ref[...], g_ref[...])
    o_ref[...] = _dot(h.astype(BF16), w_ref[...])


def norm_linear(x, g, w, tm):
    n, d = x.shape
    wd = w.shape[1]
    return pl.pallas_call(
        _norm_linear_kernel,
        grid=(n // tm,),
        in_specs=[pl.BlockSpec((tm, d), lambda i: (i, 0)),
                  pl.BlockSpec((1, d), lambda i: (0, 0)),
                  pl.BlockSpec((d, wd), lambda i: (0, 0))],
        out_specs=pl.BlockSpec((tm, wd), lambda i: (i, 0)),
        out_shape=jax.ShapeDtypeStruct((n, wd), F32),
        compiler_params=_cp("parallel"),
        name="norm_linear",
    )(x, g.reshape(1, d), w)


def _linear_residual_kernel(x_ref, a_ref, w_ref, o_ref):
    o_ref[...] = x_ref[...] + _dot(a_ref[...].astype(BF16), w_ref[...])


def linear_residual(x, a, w, tm):
    n, d = x.shape
    k = a.shape[1]
    return pl.pallas_call(
        _linear_residual_kernel,
        grid=(n // tm,),
        in_specs=[pl.BlockSpec((tm, d), lambda i: (i, 0)),
                  pl.BlockSpec((tm, k), lambda i: (i, 0)),
                  pl.BlockSpec((k, d), lambda i: (0, 0))],
        out_specs=pl.BlockSpec((tm, d), lambda i: (i, 0)),
        out_shape=jax.ShapeDtypeStruct((n, d), F32),
        compiler_params=_cp("parallel"),
        name="linear_residual",
    )(x, a, w)


def _mlp_kernel(x_ref, g_ref, wu_ref, wd_ref, o_ref, h_scr):
    j = pl.program_id(1)

    @pl.when(j == 0)
    def _():
        x = x_ref[...]
        h_scr[...] = _rms_rows(x, g_ref[...]).astype(BF16)
        o_ref[...] = x

    u = _dot(h_scr[...], wu_ref[...])
    a = jnp.square(jnp.maximum(u, 0.0)).astype(BF16)
    o_ref[...] += _dot(a, wd_ref[...])


def mlp(x, g, w_up, w_down, tm, tf):
    n, d = x.shape
    ff = w_up.shape[1]
    return pl.pallas_call(
        _mlp_kernel,
        grid=(n // tm, ff // tf),
        in_specs=[pl.BlockSpec((tm, d), lambda i, j: (i, 0)),
                  pl.BlockSpec((1, d), lambda i, j: (0, 0)),
                  pl.BlockSpec((d, tf), lambda i, j: (0, j)),
                  pl.BlockSpec((tf, d), lambda i, j: (j, 0))],
        out_specs=pl.BlockSpec((tm, d), lambda i, j: (i, 0)),
        out_shape=jax.ShapeDtypeStruct((n, d), F32),
        scratch_shapes=[pltpu.VMEM((tm, d), BF16)],
        compiler_params=_cp("parallel", "arbitrary"),
        name="mlp",
    )(x, g.reshape(1, d), w_up, w_down)


def _final_norm_kernel(x_ref, g_ref, o_ref):
    o_ref[...] = _rms_rows(x_ref[...], g_ref[...])


def final_norm(x, g, tm):
    n, d = x.shape
    return pl.pallas_call(
        _final_norm_kernel,
        grid=(n // tm,),
        in_specs=[pl.BlockSpec((tm, d), lambda i: (i, 0)), pl.BlockSpec((1, d), lambda i: (0, 0))],
        out_specs=pl.BlockSpec((tm, d), lambda i: (i, 0)),
        out_shape=jax.ShapeDtypeStruct((n, d), F32),
        compiler_params=_cp("parallel"),
        name="final_norm",
    )(x, g.reshape(1, d))


def _rope(x, c, s):
    w = x.shape[-1]
    lane = lax.broadcasted_iota(I32, x.shape, 1)
    first = (lane % HD_A) < (HD_A // 2)
    partner = jnp.where(first, pltpu.roll(x, w - HD_A // 2, 1), pltpu.roll(x, HD_A // 2, 1))
    return x * c + partner * s


def _dsa_prep_kernel(a_ref, c_ref, s_ref, gk_ref, q_ref, k_ref, kb_ref, qi_ref, tail_ref, ki2_ref):
    c = c_ref[...]
    s = s_ref[...]
    qa = a_ref[:, 0:W_A]
    ka = a_ref[:, W_A:2 * W_A]
    qi = a_ref[:, 3 * W_A:4 * W_A]
    t = a_ref[:, 4 * W_A:4 * W_A + LANES]
    q_ref[...] = (_rope(qa, c, s) * (HD_A ** -0.5)).astype(BF16)
    kr = _rope(ka, c, s)
    k_ref[...] = kr
    kb_ref[...] = kr.astype(BF16)
    qi_ref[...] = _rope(qi, c, s).astype(BF16)
    lane = lax.broadcasted_iota(I32, t.shape, 1)
    is_k = lane < D_IDX
    ms = jnp.sum(jnp.where(is_k, t * t, 0.0), axis=-1, keepdims=True) * (1.0 / D_IDX)
    kin = t * lax.rsqrt(ms + EPS) * gk_ref[...]
    kir = _rope(kin, c[:, :LANES], s[:, :LANES])
    is_w = jnp.logical_and(lane >= D_IDX, lane < D_IDX + H_IDX)
    tail_ref[...] = kir + jnp.where(is_w, t * (H_IDX ** -0.5 * D_IDX ** -0.5), 0.0)
    ki2_ref[...] = (kir + pltpu.roll(kir, D_IDX, 1)).astype(BF16)


def dsa_prep(a_raw, cos, sin, gk, tm, n_tab_tiles):
    n = a_raw.shape[0]
    row = lambda w: pl.BlockSpec((tm, w), lambda i: (i, 0))
    tab = pl.BlockSpec((tm, W_A), lambda i: (i % n_tab_tiles, 0))
    sds = lambda w, dt: jax.ShapeDtypeStruct((n, w), dt)
    return pl.pallas_call(
        _dsa_prep_kernel,
        grid=(n // tm,),
        in_specs=[row(WG_A), tab, tab, pl.BlockSpec((1, LANES), lambda i: (0, 0))],
        out_specs=[row(W_A), row(W_A), row(W_A), row(W_A), row(LANES), row(LANES)],
        out_shape=[sds(W_A, BF16), sds(W_A, F32), sds(W_A, BF16), sds(W_A, BF16),
                   sds(LANES, F32), sds(LANES, BF16)],
        compiler_params=_cp("parallel"),
        name="dsa_prep",
    )(a_raw, cos, sin, gk)


def _v_chunks_kernel(a_ref, o_ref):
    o_ref[0] = a_ref[...].T.astype(BF16)


def v_chunks_t(a_raw):
    n = a_raw.shape[0]
    return pl.pallas_call(
        _v_chunks_kernel,
        grid=(n // SC,),
        in_specs=[pl.BlockSpec((SC, W_A), lambda i: (i, 2))],
        out_specs=pl.BlockSpec((1, W_A, SC), lambda i: (i, 0, 0)),
        out_shape=jax.ShapeDtypeStruct((n // SC, W_A, SC), BF16),
        compiler_params=_cp("parallel"),
        name="v_chunks_t",
    )(a_raw)


def _sort_key(x):
    i = pltpu.bitcast(x, I32)
    return i ^ (lax.shift_right_arithmetic(i, 31) & 0x7FFFFFFF)


def _kth_largest_key(count_ge, shape):
    def cond(s):
        return jnp.logical_and(s[0] < 32, s[3] == 0)

    def body(s):
        b, t, done, _ = s
        cand = t + lax.shift_left(jnp.int32(1), 31 - b)
        n = count_ge(cand)
        done = jnp.where(n == TOPK, 1, done)
        return b + 1, jnp.where(n >= TOPK, cand, t), done, jnp.min(done)

    init = (jnp.int32(0), jnp.full(shape, -2 ** 31, I32), jnp.zeros(shape, I32), jnp.int32(0))
    return lax.while_loop(cond, body, init)[1]


def _dsa_prompt_kernel(q_ref, qi_ref, tail_ref, ki2_ref, k_ref, vt_ref, o_ref,
                       qis_scr, qm_scr, key_scr, acc_scr):
    i = pl.program_id(1)
    t0 = i * TQ
    nchunk = (t0 + TQ + SC - 1) // SC
    lane = lax.broadcasted_iota(I32, (TQ, LANES), 1)
    half = lane // HD_A

    for h in range(H_A):
        p, hl = h // 2, h % 2
        sl = slice(p * LANES, (p + 1) * LANES)
        qis_scr[h * TQ:(h + 1) * TQ, :] = jnp.where(half == hl, qi_ref[:, sl].astype(F32), 0.0).astype(BF16)
        qm_scr[h * TQ:(h + 1) * TQ, :] = jnp.where(half == hl, q_ref[:, sl].astype(F32), 0.0).astype(BF16)
    w_t = tail_ref[...].T

    key_s = lax.broadcasted_iota(I32, (SC, TQ), 0)
    qry_t = t0 + lax.broadcasted_iota(I32, (SC, TQ), 1)

    def score_chunk(c, carry):
        base = pl.multiple_of(c * SC, SC)
        kc = ki2_ref[pl.ds(base, SC), :]
        acc = jnp.zeros((SC, TQ), F32)
        for h2 in range(H_IDX // 2):
            s2 = _dot_nt(kc, qis_scr[2 * h2 * TQ:(2 * h2 + 2) * TQ, :])
            for hl in range(2):
                h = 2 * h2 + hl
                acc = acc + w_t[D_IDX + h:D_IDX + h + 1, :] * jnp.maximum(s2[:, hl * TQ:(hl + 1) * TQ], 0.0)
        sc = jnp.where(key_s + base <= qry_t, acc, NEG_INF)
        key_scr[c] = _sort_key(sc)
        return carry

    lax.fori_loop(0, nchunk, score_chunk, 0)

    def count(pred):
        def body(c, part):
            hit = jnp.where(pred(key_scr[c], c), 1, 0)
            return part + jnp.sum(hit.reshape(SC // SUBLANES, SUBLANES, TQ), axis=0)
        part = lax.fori_loop(0, nchunk, body, jnp.zeros((SUBLANES, TQ), I32))
        return jnp.sum(part, axis=0, keepdims=True)

    thr = _kth_largest_key(lambda cand: count(lambda kx, c: kx >= cand), (1, TQ))
    n_ge = count(lambda kx, c: kx >= thr)
    excess = jnp.logical_and(n_ge > TOPK, thr > KEY_NEG_INF)

    @pl.when(jnp.max(excess.astype(I32)) > 0)
    def _():
        need = TOPK - count(lambda kx, c: kx > thr)

        def body(b, y):
            cand = y + lax.shift_left(jnp.int32(1), 13 - b)
            f = count(lambda kx, c: jnp.logical_and(kx == thr, key_s + c * SC < cand))
            return jnp.where(f < need, cand, y)

        cut = lax.fori_loop(0, 14, body, jnp.zeros((1, TQ), I32)) + 1
        cut = jnp.where(excess, cut, jnp.int32(2 ** 30))

        def drop(c, carry):
            kx = key_scr[c]
            dropped = jnp.logical_and(kx == thr, key_s + c * SC >= cut)
            key_scr[c] = jnp.where(dropped, KEY_NEG_INF, kx)
            return carry

        lax.fori_loop(0, nchunk, drop, 0)

    def to_bias(c, carry):
        sel = jnp.logical_and(key_scr[c] >= thr, key_s + c * SC <= qry_t)
        key_scr[c] = pltpu.bitcast(jnp.where(sel, 0.0, NEG_INF), I32)
        return carry

    lax.fori_loop(0, nchunk, to_bias, 0)

    npair = H_A // 2
    acc_scr[...] = jnp.zeros(acc_scr.shape, F32)

    def attend_chunk(c, carry):
        base = pl.multiple_of(c * SC, SC)
        bias = pltpu.bitcast(key_scr[c], F32)
        bias2 = jnp.concatenate([bias, bias], axis=1)
        pairs = [slice(p * LANES, (p + 1) * LANES) for p in range(npair)]
        lgs = [_dot_nt(k_ref[pl.ds(base, SC), pairs[p]], qm_scr[2 * p * TQ:(2 * p + 2) * TQ, :])
               for p in range(npair)]
        new, prs, alphas = [], [], []
        for p in range(npair):
            m_old, l_old = carry[p]
            x = lgs[p] + bias2
            m_new = jnp.maximum(m_old, jnp.max(x, axis=0, keepdims=True))
            m_safe = jnp.where(m_new == NEG_INF, 0.0, m_new)
            pr = jnp.exp(x - m_safe)
            alpha = jnp.exp(m_old - m_safe)
            new.append((m_new, alpha * l_old + jnp.sum(pr, axis=0, keepdims=True)))
            prs.append(pr.astype(BF16))
            alphas.append(alpha)
        pvs = [_dot(vt_ref[c, pairs[p], :], prs[p]) for p in range(npair)]
        for p in range(npair):
            acc_scr[p] = alphas[p] * acc_scr[p] + pvs[p]
        return tuple(new)

    init = tuple((jnp.full((1, 2 * TQ), NEG_INF, F32), jnp.zeros((1, 2 * TQ), F32)) for _ in range(npair))
    fin = lax.fori_loop(0, nchunk, attend_chunk, init)
    for p in range(npair):
        l_fin = fin[p][1]
        acc = acc_scr[p]
        o_t = jnp.concatenate([acc[0:HD_A, 0:TQ] / l_fin[:, 0:TQ],
                               acc[HD_A:2 * HD_A, TQ:2 * TQ] / l_fin[:, TQ:2 * TQ]], axis=0)
        o_ref[:, p * LANES:(p + 1) * LANES] = o_t.T


def dsa_prompt(q, qi, tail, ki2, kb, vt):
    b, t, _ = q.shape
    blk = lambda w: pl.BlockSpec((None, TQ, w), lambda bi, i: (bi, i, 0))
    full = lambda w: pl.BlockSpec((None, t, w), lambda bi, i: (bi, 0, 0), pipeline_mode=pl.Buffered(1))
    return pl.pallas_call(
        _dsa_prompt_kernel,
        grid=(b, t // TQ),
        in_specs=[blk(W_A), blk(W_A), blk(LANES), full(LANES), full(W_A),
                  pl.BlockSpec((None, t // SC, W_A, SC), lambda bi, i: (bi, 0, 0, 0),
                               pipeline_mode=pl.Buffered(1))],
        out_specs=blk(W_A),
        out_shape=jax.ShapeDtypeStruct((b, t, W_A), F32),
        scratch_shapes=[pltpu.VMEM((H_IDX * TQ, LANES), BF16),
                        pltpu.VMEM((H_A * TQ, LANES), BF16),
                        pltpu.VMEM((t // SC, SC, TQ), I32),
                        pltpu.VMEM((H_A // 2, LANES, 2 * TQ), F32)],
        compiler_params=_cp("parallel", "arbitrary"),
        name="dsa_prompt",
    )(q, qi, tail, ki2, kb, vt)


PG = 8


def _dsa_sample_score_kernel(pt_ref, qi_ref, w_ref, *refs):
    page_refs, o_ref = refs[:PG], refs[PG]
    j = pl.program_id(1)
    qh = qi_ref[0]
    w_col = w_ref[0]
    for g in range(PG):
        kpt = page_refs[g][...].astype(BF16)
        s = jnp.maximum(_dot(qh, kpt), 0.0)
        o_ref[0, :, pl.ds(pl.multiple_of((j * PG + g) * PAGE, PAGE), PAGE)] = jnp.sum(
            w_col * s, axis=0, keepdims=True)


def dsa_sample_scores(page_table, layer, qi, tail, pool_kidx_t):
    b, n_pages = page_table.shape
    page_spec = lambda g: pl.BlockSpec((None, None, D_IDX, PAGE),
                                       lambda bi, j, pt: (layer, pt[bi, j * PG + g], 0, 0))
    return pl.pallas_call(
        _dsa_sample_score_kernel,
        grid_spec=pltpu.PrefetchScalarGridSpec(
            num_scalar_prefetch=1,
            grid=(b, n_pages // PG),
            in_specs=[pl.BlockSpec((1, H_IDX, D_IDX), lambda bi, j, pt: (bi, 0, 0)),
                      pl.BlockSpec((1, H_IDX, 1), lambda bi, j, pt: (bi, 0, 0))]
                     + [page_spec(g) for g in range(PG)],
            out_specs=pl.BlockSpec((1, 1, n_pages * PAGE), lambda bi, j, pt: (bi, 0, 0)),
        ),
        out_shape=jax.ShapeDtypeStruct((b, 1, n_pages * PAGE), F32),
        compiler_params=_cp("parallel", "arbitrary"),
        name="dsa_sample_scores",
    )(page_table, qi.reshape(b, H_IDX, D_IDX), tail[:, D_IDX:D_IDX + H_IDX].reshape(b, H_IDX, 1),
      *([pool_kidx_t] * PG))


def _dsa_sample_select_kernel(sc_ref, qi_ref, tail_ref, bias_ref, newsel_ref):
    sc = sc_ref[...]
    nb, past = sc.shape
    qi = qi_ref[...]
    ki = tail_ref[...]
    lane = lax.broadcasted_iota(I32, ki.shape, 1)
    kidx = jnp.where(lane < D_IDX, ki, 0.0)
    kidx = (kidx + pltpu.roll(kidx, D_IDX, 1)).astype(BF16).astype(F32)
    sc_new = jnp.zeros((nb, 1), F32)
    for h in range(H_IDX):
        p, hl = h // 2, h % 2
        qh = qi[:, p * LANES:(p + 1) * LANES].astype(F32)
        d = jnp.sum(jnp.where(lane // D_IDX == hl, qh * kidx, 0.0), axis=1, keepdims=True)
        w = jnp.sum(jnp.where(lane == D_IDX + h, ki, 0.0), axis=1, keepdims=True)
        sc_new = sc_new + w * jnp.maximum(d, 0.0)
    key = _sort_key(sc + 0.0)
    key_new = _sort_key(sc_new + 0.0)

    def count_ge(cand):
        return (jnp.sum(jnp.where(key >= cand, 1, 0), axis=1, keepdims=True)
                + jnp.where(key_new >= cand, 1, 0))

    thr = _kth_largest_key(count_ge, (nb, 1))
    n_gt = jnp.sum(jnp.where(key > thr, 1, 0), axis=1, keepdims=True) + jnp.where(key_new > thr, 1, 0)
    need = TOPK - n_gt
    eq = key == thr
    pos = lax.broadcasted_iota(I32, sc.shape, 1)

    def body(b, y):
        cand = y + lax.shift_left(jnp.int32(1), 13 - b)
        f = jnp.sum(jnp.where(jnp.logical_and(eq, pos < cand), 1, 0), axis=1, keepdims=True)
        return jnp.where(f < need, cand, y)

    cut = lax.fori_loop(0, 14, body, jnp.zeros((nb, 1), I32)) + 1
    sel = jnp.logical_or(key > thr, jnp.logical_and(eq, pos < cut))
    n_sel = jnp.sum(jnp.where(sel, 1, 0), axis=1, keepdims=True)
    bias_ref[...] = jnp.where(sel, 0.0, NEG_INF)
    new_sel = jnp.logical_or(key_new > thr, jnp.logical_and(key_new == thr, n_sel < TOPK))
    newsel_ref[...] = jnp.broadcast_to(jnp.where(new_sel, 0.0, NEG_INF), newsel_ref.shape)


def dsa_sample_select(sc, qi, tail):
    b, past = sc.shape
    return pl.pallas_call(
        _dsa_sample_select_kernel,
        out_shape=[jax.ShapeDtypeStruct((b, past), F32), jax.ShapeDtypeStruct((b, LANES), F32)],
        compiler_params=pltpu.CompilerParams(vmem_limit_bytes=VMEM_LIMIT),
        name="dsa_sample_select",
    )(sc, qi, tail)


def _dsa_sample_attn_kernel(pt_ref, q_ref, kn_ref, vn_ref, bias_ref, nsel_ref, *refs):
    k_refs, v_refs, o_ref = refs[:PG], refs[PG:2 * PG], refs[2 * PG]
    m_scr, l_scr, acc_scr = refs[2 * PG + 1:]
    j = pl.program_id(1)
    qb = jnp.broadcast_to(q_ref[0], (H_A, HD_A, PAGE))
    lane0 = lax.broadcasted_iota(I32, (H_A, 1, PAGE), 2) == 0

    @pl.when(j == 0)
    def _():
        lg = jnp.sum(q_ref[0] * kn_ref[0], axis=1, keepdims=True) + nsel_ref[0][:, 0:1]
        m_scr[...] = lg
        pr = jnp.where(jnp.logical_and(lane0, lg > NEG_INF), 1.0, 0.0)
        l_scr[...] = pr
        acc_scr[...] = pr * vn_ref[0]

    for g in range(PG):
        bias = bias_ref[0, :, pl.ds(pl.multiple_of((j * PG + g) * PAGE, PAGE), PAGE)]
        x = jnp.sum(qb * k_refs[g][...], axis=1, keepdims=True) + bias
        m_old = m_scr[...]
        m_new = jnp.maximum(m_old, jnp.max(x, axis=2, keepdims=True))
        m_safe = jnp.where(m_new == NEG_INF, 0.0, m_new)
        pr = jnp.exp(x - m_safe)
        alpha = jnp.exp(m_old - m_safe)
        l_scr[...] = alpha * l_scr[...] + pr
        acc_scr[...] = alpha * acc_scr[...] + pr * v_refs[g][...]
        m_scr[...] = m_new

    @pl.when(j == pl.num_programs(1) - 1)
    def _():
        o_ref[0] = (jnp.sum(acc_scr[...], axis=2, keepdims=True)
                    / jnp.sum(l_scr[...], axis=2, keepdims=True))


def dsa_sample_attend(page_table, layer, q, k_new, v_new, bias, new_sel, cache_kt, cache_vt):
    b, n_pages = page_table.shape
    page_spec = lambda g: pl.BlockSpec((None, None, H_A, HD_A, PAGE),
                                       lambda bi, j, pt: (layer, pt[bi, j * PG + g], 0, 0, 0))
    cols = pl.BlockSpec((1, H_A, HD_A, 1), lambda bi, j, pt: (bi, 0, 0, 0))
    row = lambda w: pl.BlockSpec((1, 1, w), lambda bi, j, pt: (bi, 0, 0))
    col = lambda a: a.astype(F32).reshape(b, H_A, HD_A, 1)
    return pl.pallas_call(
        _dsa_sample_attn_kernel,
        grid_spec=pltpu.PrefetchScalarGridSpec(
            num_scalar_prefetch=1,
            grid=(b, n_pages // PG),
            in_specs=[cols, cols, cols, row(n_pages * PAGE), row(LANES)]
                     + [page_spec(g) for g in range(PG)] * 2,
            out_specs=cols,
            scratch_shapes=[pltpu.VMEM((H_A, 1, 1), F32), pltpu.VMEM((H_A, 1, PAGE), F32),
                            pltpu.VMEM((H_A, HD_A, PAGE), F32)],
        ),
        out_shape=jax.ShapeDtypeStruct((b, H_A, HD_A, 1), F32),
        compiler_params=_cp("parallel", "arbitrary"),
        name="dsa_sample_attend",
    )(page_table, col(q), col(k_new), col(v_new),
      bias.reshape(b, 1, n_pages * PAGE), new_sel.reshape(b, 1, LANES),
      *([cache_kt] * PG), *([cache_vt] * PG))


def _block_tri(n, blk, strict_upper):
    r = lax.broadcasted_iota(I32, (n, n), 0)
    c = lax.broadcasted_iota(I32, (n, n), 1)
    same = (r // blk) == (c // blk)
    tri = (c > r) if strict_upper else (c <= r)
    return jnp.where(jnp.logical_and(same, tri), 1.0, 0.0).astype(F32)


def _split_dot(x, w):
    hi = x.astype(BF16)
    lo = (x - hi.astype(F32)).astype(BF16)
    return _dot(hi, w) + _dot(lo, w)


def _hgrn_prompt_kernel(b_ref, lb_ref, llb_ref, o_ref, s_ref, st_scr):
    i = pl.program_id(1)
    tm = b_ref.shape[0]
    nsub = tm // SUB

    @pl.when(i == 0)
    def _():
        st_scr[...] = jnp.zeros(st_scr.shape, F32)

    lb = lb_ref[...]
    fb = b_ref[:, W_B:2 * W_B]
    q = _silu(b_ref[:, 0:W_B])
    v = b_ref[:, 2 * W_B:3 * W_B]
    kk = (1.0 - lb) * jax.nn.sigmoid(-fb)
    la = llb_ref[0:1, :]
    lbb = llb_ref[1:2, :] + (jnp.minimum(fb, 0.0) - jnp.log1p(jnp.exp(-jnp.abs(fb))))
    logf = jnp.maximum(la, lbb) + jnp.log1p(jnp.exp(-jnp.abs(la - lbb)))

    g = _dot_exact(_block_tri(tm, SUB, False), logf)
    r = _dot_exact(_block_tri(tm, SUB, True), logf)
    qg = (q * jnp.exp(g)).astype(BF16)
    kg = kk * jnp.exp(r)
    eg_last = jnp.exp(g + r)

    rows = lax.broadcasted_iota(I32, (tm, W_B), 0) % SUB
    hr = lax.broadcasted_iota(I32, (W_B, W_B), 0) // DK_B
    hc = lax.broadcasted_iota(I32, (W_B, W_B), 1) // DK_B
    head_ones = jnp.where(hr == hc, 1.0, 0.0).astype(BF16)
    o = jnp.zeros((tm, W_B), F32)
    for j in range(SUB):
        if j == 0:
            w = q * kk
            vj = v
        else:
            dec = jnp.exp(jnp.where(rows >= j, g - pltpu.roll(g, j, 0), NEG_INF))
            w = q * pltpu.roll(kk, j, 0) * dec
            vj = pltpu.roll(v, j, 0)
        o = o + _split_dot(w, head_ones) * vj

    rowi = lax.broadcasted_iota(I32, (tm, DK_B), 0) // SUB
    for h in range(H_B):
        sl = slice(h * DK_B, (h + 1) * DK_B)
        vt = v[:, sl].T.astype(BF16)
        qg_h = qg[:, sl]
        kg_h = kg[:, sl]
        st = st_scr[h]
        oh = jnp.zeros((tm, DV_B), F32)
        for c in range(nsub):
            inc = rowi == c
            oh = oh + jnp.where(inc, _dot_nt(qg_h, st.astype(BF16)), 0.0)
            st = st * eg_last[c * SUB:c * SUB + 1, sl] + _dot(vt, jnp.where(inc, kg_h, 0.0).astype(BF16))
        st_scr[h] = st
        o_ref[:, sl] = o[:, sl] + oh

    @pl.when(i == pl.num_programs(1) - 1)
    def _():
        for h in range(H_B):
            s_ref[h] = st_scr[h].T


def hgrn_prompt(b_raw, lb):
    b, t, _ = b_raw.shape
    tm = 128
    llb = jnp.stack([jnp.log(lb), jnp.log1p(-lb)])
    return pl.pallas_call(
        _hgrn_prompt_kernel,
        grid=(b, t // tm),
        in_specs=[pl.BlockSpec((None, tm, WG_B), lambda bi, i: (bi, i, 0)),
                  pl.BlockSpec((1, W_B), lambda bi, i: (0, 0)),
                  pl.BlockSpec((2, W_B), lambda bi, i: (0, 0))],
        out_specs=[pl.BlockSpec((None, tm, W_B), lambda bi, i: (bi, i, 0)),
                   pl.BlockSpec((None, H_B, DK_B, DV_B), lambda bi, i: (bi, 0, 0, 0))],
        out_shape=[jax.ShapeDtypeStruct((b, t, W_B), F32),
                   jax.ShapeDtypeStruct((b, H_B, DK_B, DV_B), F32)],
        scratch_shapes=[pltpu.VMEM((H_B, DV_B, DK_B), F32)],
        compiler_params=_cp("parallel", "arbitrary"),
        name="hgrn_prompt",
    )(b_raw, lb.reshape(1, W_B), llb)


def _hgrn_sample_kernel(qb_ref, fb_ref, ib_ref, lb_ref, s_ref, o_ref, sn_ref):
    lb = lb_ref[...]
    fb = fb_ref[0]
    f = lb + (1.0 - lb) * jax.nn.sigmoid(fb)
    kk = (1.0 - lb) * jax.nn.sigmoid(-fb)
    s_new = f * s_ref[0] + kk * ib_ref[0]
    sn_ref[0] = s_new
    o_ref[0] = jnp.sum(_silu(qb_ref[0]) * s_new, axis=1, keepdims=True)


def hgrn_sample(b_raw, lb, state):
    b = b_raw.shape[0]
    col = lambda a: a.reshape(b, H_B, DK_B, 1)
    cspec = pl.BlockSpec((1, H_B, DK_B, 1), lambda i: (i, 0, 0, 0))
    rspec = pl.BlockSpec((1, H_B, 1, DV_B), lambda i: (i, 0, 0, 0))
    sspec = pl.BlockSpec((1, H_B, DK_B, DV_B), lambda i: (i, 0, 0, 0))
    o, s_new = pl.pallas_call(
        _hgrn_sample_kernel,
        grid=(b,),
        in_specs=[cspec, cspec, rspec, pl.BlockSpec((H_B, DK_B, 1), lambda i: (0, 0, 0)), sspec],
        out_specs=[rspec, sspec],
        out_shape=[jax.ShapeDtypeStruct((b, H_B, 1, DV_B), F32),
                   jax.ShapeDtypeStruct((b, H_B, DK_B, DV_B), F32)],
        compiler_params=_cp("parallel"),
        name="hgrn_sample",
    )(col(b_raw[:, 0:W_B]), col(b_raw[:, W_B:2 * W_B]), b_raw[:, 2 * W_B:3 * W_B].reshape(b, H_B, 1, DV_B),
      lb.reshape(H_B, DK_B, 1), state)
    return o.reshape(b, W_B), s_new


def _head_sum_matrix():
    r = lax.broadcasted_iota(I32, (W_C, W_C), 0) // DK_C
    c = lax.broadcasted_iota(I32, (W_C, W_C), 1) // DK_C
    return jnp.where(r == c, 1.0, 0.0).astype(F32)


def _gdn_prep_kernel(c_ref, halo_ref, buf_ref, w_ref, al_ref, dt_ref, kq_ref, v_ref, eg_ref, be_ref):
    i = pl.program_id(1)
    tm = c_ref.shape[0]
    prev = jnp.where(i == 0, buf_ref[...], halo_ref[:, 0:CONV_DIM])
    win = jnp.concatenate([prev, c_ref[:, 0:CONV_DIM]], axis=0)
    y = jnp.zeros((tm, CONV_DIM), F32)
    for tap in range(CONV_W):
        off = 8 - (CONV_W - 1) + tap
        y = y + win[off:off + tm, :] * w_ref[tap:tap + 1, :]
    y = _silu(y)
    ones = _head_sum_matrix().astype(BF16)
    qc = y[:, 0:W_C]
    kc = y[:, W_C:2 * W_C]
    qn = qc * lax.rsqrt(_split_dot(qc * qc, ones) + EPS) * (DK_C ** -0.5)
    kn = kc * lax.rsqrt(_split_dot(kc * kc, ones) + EPS)
    kbits = pltpu.bitcast(kn.astype(BF16).astype(F32), I32) & jnp.int32(-65536)
    qbits = lax.shift_right_logical(pltpu.bitcast(qn.astype(BF16).astype(F32), I32), 16)
    kq_ref[...] = kbits | qbits
    v_ref[...] = y[:, 2 * W_C:3 * W_C]
    tail = c_ref[:, CONV_DIM + W_C:CONV_DIM + W_C + LANES]
    r = lax.broadcasted_iota(I32, (LANES, W_C), 0)
    cc = lax.broadcasted_iota(I32, (LANES, W_C), 1) // DK_C
    bcl = _dot_exact(tail, jnp.where(r == cc, 1.0, 0.0).astype(F32))
    acl = _dot_exact(tail, jnp.where(r == cc + H_C, 1.0, 0.0).astype(F32))
    be_ref[...] = jax.nn.sigmoid(bcl)
    x = acl + dt_ref[...]
    softplus = jnp.maximum(x, 0.0) + jnp.log1p(jnp.exp(-jnp.abs(x)))
    eg_ref[...] = jnp.exp(-jnp.exp(al_ref[...]) * softplus)


def gdn_prep(c_raw, buf8, conv_w, a_log_l, dt_bias_l, tm):
    b, t, _ = c_raw.shape
    blk = lambda w: pl.BlockSpec((None, tm, w), lambda bi, i: (bi, i, 0))
    sds = jax.ShapeDtypeStruct((b, t, W_C), F32)
    return pl.pallas_call(
        _gdn_prep_kernel,
        grid=(b, t // tm),
        in_specs=[blk(WG_C),
                  pl.BlockSpec((None, 8, WG_C), lambda bi, i: (bi, jnp.maximum(i * (tm // 8) - 1, 0), 0)),
                  pl.BlockSpec((None, 8, CONV_DIM), lambda bi, i: (bi, 0, 0)),
                  pl.BlockSpec((CONV_W, CONV_DIM), lambda bi, i: (0, 0)),
                  pl.BlockSpec((1, W_C), lambda bi, i: (0, 0)),
                  pl.BlockSpec((1, W_C), lambda bi, i: (0, 0))],
        out_specs=[blk(W_C)] * 4,
        out_shape=[jax.ShapeDtypeStruct((b, t, W_C), I32), sds, sds, sds],
        compiler_params=_cp("parallel", "arbitrary"),
        name="gdn_prep",
    )(c_raw, c_raw, buf8, conv_w, a_log_l, dt_bias_l)


def _gdn_prompt_kernel(kq_ref, v_ref, eg_ref, be_ref, o_ref, s_ref, z_scr):
    i = pl.program_id(1)

    @pl.when(i == 0)
    def _():
        z_scr[...] = jnp.zeros(z_scr.shape, F32)

    ri = lax.broadcasted_iota(I32, (DK_C, LANES), 0)
    lj = lax.broadcasted_iota(I32, (DK_C, LANES), 1) % DV_C
    keep = lj >= ri

    def rot(r):
        a = pltpu.roll(jnp.broadcast_to(r, (DK_C, LANES)), 0, 1, stride=1, stride_axis=0)
        return jnp.where(keep, a, pltpu.roll(a, DV_C, 1))

    def step(t, carry):
        kqrow = kq_ref[pl.ds(t, 1), :]
        vrow = v_ref[pl.ds(t, 1), :]
        egrow = eg_ref[pl.ds(t, 1), :]
        berow = be_ref[pl.ds(t, 1), :]
        outs = []
        for p in range(H_C // 2):
            sl = slice(p * LANES, (p + 1) * LANES)
            word = rot(kqrow[:, sl])
            kk = pltpu.bitcast(word & jnp.int32(-65536), F32)
            qq = pltpu.bitcast(lax.shift_left(word, 16), F32)
            z = z_scr[p]
            ks = jnp.sum(kk * z, axis=0, keepdims=True)
            u = berow[:, sl] * (vrow[:, sl] - egrow[:, sl] * ks)
            z = egrow[:, sl] * z + kk * u
            z_scr[p] = z
            outs.append(jnp.sum(qq * z, axis=0, keepdims=True))
        o_ref[pl.ds(t, 1), :] = jnp.concatenate(outs, axis=1)
        return carry

    lax.fori_loop(0, kq_ref.shape[0], step, 0, unroll=2)

    @pl.when(i == pl.num_programs(1) - 1)
    def _():
        s_ref[...] = z_scr[...]


def _unrotate(z):
    n = z.shape[-1]
    d = jnp.arange(n)[:, None]
    v = jnp.arange(n)[None, :]
    idx = jnp.broadcast_to((v - d) % n, z.shape)
    return jnp.take_along_axis(z, idx, axis=-2)


def gdn_prompt(kq, v, eg, be):
    b, t, _ = v.shape
    blk = pl.BlockSpec((None, TB, W_C), lambda bi, i: (bi, i, 0))
    o, z = pl.pallas_call(
        _gdn_prompt_kernel,
        grid=(b, t // TB),
        in_specs=[blk] * 4,
        out_specs=[blk, pl.BlockSpec((None, H_C // 2, DK_C, LANES), lambda bi, i: (bi, 0, 0, 0))],
        out_shape=[jax.ShapeDtypeStruct((b, t, W_C), F32),
                   jax.ShapeDtypeStruct((b, H_C // 2, DK_C, LANES), F32)],
        scratch_shapes=[pltpu.VMEM((H_C // 2, DK_C, LANES), F32)],
        compiler_params=_cp("parallel", "arbitrary"),
        name="gdn_prompt",
    )(kq, v, eg, be)
    z = z.reshape(b, H_C // 2, DK_C, 2, DV_C).transpose(0, 1, 3, 2, 4).reshape(b, H_C, DK_C, DV_C)
    return o, _unrotate(z)


def _gdn_sample_kernel(qk_ref, vv_ref, wqk_ref, wv_ref, bc_ref, ac_ref, al_ref, dt_ref, s_ref, o_ref, sn_ref):
    yqk = jnp.zeros(qk_ref.shape[2:], F32)
    yv = jnp.zeros(vv_ref.shape[2:], F32)
    for tap in range(CONV_W):
        yqk = yqk + qk_ref[0, tap] * wqk_ref[tap]
        yv = yv + vv_ref[0, tap] * wv_ref[tap]
    yqk = _silu(yqk)
    vrow = _silu(yv)
    yq, yk = yqk[0:H_C], yqk[H_C:2 * H_C]
    qcol = yq * lax.rsqrt(jnp.sum(yq * yq, axis=1, keepdims=True) + EPS) * (DK_C ** -0.5)
    kcol = yk * lax.rsqrt(jnp.sum(yk * yk, axis=1, keepdims=True) + EPS)
    beta = jax.nn.sigmoid(bc_ref[0])
    x = ac_ref[0] + dt_ref[...]
    eg = jnp.exp(-jnp.exp(al_ref[...]) * (jnp.maximum(x, 0.0) + jnp.log1p(jnp.exp(-jnp.abs(x)))))
    s = s_ref[0]
    ks = jnp.sum(kcol * s, axis=1, keepdims=True)
    u = beta * (vrow - eg * ks)
    s_new = eg * s + kcol * u
    sn_ref[0] = s_new
    o_ref[0] = jnp.sum(qcol * s_new, axis=1, keepdims=True)


def gdn_sample(c_raw, conv_buf, conv_w, a_log_l, dt_bias_l, state):
    b = c_raw.shape[0]
    taps = jnp.concatenate([conv_buf, c_raw[:, None, 0:CONV_DIM]], axis=1)
    qk = taps[:, :, 0:2 * W_C].reshape(b, CONV_W, 2 * H_C, DK_C, 1)
    vv = taps[:, :, 2 * W_C:].reshape(b, CONV_W, H_C, 1, DV_C)
    wqk = conv_w[:, 0:2 * W_C].reshape(CONV_W, 2 * H_C, DK_C, 1)
    wv = conv_w[:, 2 * W_C:].reshape(CONV_W, H_C, 1, DV_C)
    bc = c_raw[:, CONV_DIM + W_C:CONV_DIM + W_C + H_C].reshape(b, H_C, 1, 1)
    ac = c_raw[:, CONV_DIM + W_C + H_C:CONV_DIM + W_C + 2 * H_C].reshape(b, H_C, 1, 1)
    per_seq = lambda shp: pl.BlockSpec((1,) + shp, lambda i: (i,) + (0,) * len(shp))
    const = lambda shp: pl.BlockSpec(shp, lambda i: (0,) * len(shp))
    o, s_new = pl.pallas_call(
        _gdn_sample_kernel,
        grid=(b,),
        in_specs=[per_seq((CONV_W, 2 * H_C, DK_C, 1)), per_seq((CONV_W, H_C, 1, DV_C)),
                  const((CONV_W, 2 * H_C, DK_C, 1)), const((CONV_W, H_C, 1, DV_C)),
                  per_seq((H_C, 1, 1)), per_seq((H_C, 1, 1)), const((H_C, 1, 1)), const((H_C, 1, 1)),
                  per_seq((H_C, DK_C, DV_C))],
        out_specs=[per_seq((H_C, 1, DV_C)), per_seq((H_C, DK_C, DV_C))],
        out_shape=[jax.ShapeDtypeStruct((b, H_C, 1, DV_C), F32),
                   jax.ShapeDtypeStruct((b, H_C, DK_C, DV_C), F32)],
        compiler_params=_cp("parallel"),
        name="gdn_sample",
    )(qk, vv, wqk, wv, bc, ac, a_log_l.reshape(H_C, 1, 1), dt_bias_l.reshape(H_C, 1, 1), state)
    return o.reshape(b, W_C), s_new, taps[:, 1:, :]


def _merge_kernel(x_ref, g_ref, oa_ref, ob_ref, oc_ref, gb_ref, zc_ref, gh_ref, gg_ref,
                  wa_ref, wb_ref, wc_ref, wo_ref, o_ref):
    ob = ob_ref[...]
    parts = []
    for h in range(H_B):
        seg = ob[:, h * DV_B:(h + 1) * DV_B]
        parts.append(seg * lax.rsqrt(jnp.mean(seg * seg, axis=-1, keepdims=True) + EPS))
    obn = jnp.concatenate(parts, axis=1) * gh_ref[...] * _silu(gb_ref[...])
    oc = oc_ref[...]
    ms = _split_dot(oc * oc, _head_sum_matrix().astype(BF16)) * (1.0 / DV_C)
    ocn = oc * lax.rsqrt(ms + EPS) * gg_ref[...] * _silu(zc_ref[...])
    m = (jax.nn.sigmoid(g_ref[:, 0:D_MODEL]) * _dot(oa_ref[...].astype(BF16), wa_ref[...])
         + jax.nn.sigmoid(g_ref[:, D_MODEL:2 * D_MODEL]) * _dot(obn.astype(BF16), wb_ref[...])
         + jax.nn.sigmoid(g_ref[:, 2 * D_MODEL:3 * D_MODEL]) * _dot(ocn.astype(BF16), wc_ref[...]))
    o_ref[...] = x_ref[...] + _dot(m.astype(BF16), wo_ref[...])


def merge(x, gates, oa, ob, oc, b_raw, c_raw, g_hgrn, g_gdn, wa, wb, wc, wo, tm):
    n = x.shape[0]
    row = lambda w: pl.BlockSpec((tm, w), lambda i: (i, 0))
    const = lambda r, c: pl.BlockSpec((r, c), lambda i: (0, 0))
    return pl.pallas_call(
        _merge_kernel,
        grid=(n // tm,),
        in_specs=[row(D_MODEL), row(WG_G), row(W_A), row(W_B), row(W_C),
                  pl.BlockSpec((tm, W_B), lambda i: (i, 3)),
                  pl.BlockSpec((tm, W_C), lambda i: (i, 3)),
                  const(1, W_B), const(1, W_C),
                  const(W_A, D_MODEL), const(W_B, D_MODEL), const(W_C, D_MODEL), const(D_MODEL, D_MODEL)],
        out_specs=row(D_MODEL),
        out_shape=jax.ShapeDtypeStruct((n, D_MODEL), F32),
        compiler_params=_cp("parallel"),
        name="merge",
    )(x, gates, oa, ob, oc, b_raw, c_raw, g_hgrn.reshape(1, W_B), g_gdn.reshape(1, W_C), wa, wb, wc, wo)


def _cross_prompt_kernel(x_ref, g_ref, wq_ref, mk_ref, mv_ref, wo_ref, o_ref):
    x = x_ref[...]
    q = _dot(_rms_rows(x, g_ref[...]).astype(BF16), wq_ref[...])
    outs = []
    for h in range(H_X):
        sl = slice(h * HD_X, (h + 1) * HD_X)
        lg = _dot_nt(q[:, sl].astype(BF16), mk_ref[:, sl]) * (HD_X ** -0.5)
        pr = jnp.exp(lg - jnp.max(lg, axis=-1, keepdims=True))
        pr = pr / jnp.sum(pr, axis=-1, keepdims=True)
        outs.append(_dot(pr.astype(BF16), mv_ref[:, sl]))
    o_ref[...] = x + _dot(jnp.concatenate(outs, axis=1).astype(BF16), wo_ref[...])


def cross_prompt(x, g, wq, mk, mv, wo, tm):
    b, t, d = x.shape
    const = lambda r, c: pl.BlockSpec((r, c), lambda bi, i: (0, 0))
    mem = pl.BlockSpec((None, N_MEM, W_X), lambda bi, i: (bi, 0, 0))
    return pl.pallas_call(
        _cross_prompt_kernel,
        grid=(b, t // tm),
        in_specs=[pl.BlockSpec((None, tm, d), lambda bi, i: (bi, i, 0)), const(1, d), const(d, W_X),
                  mem, mem, const(W_X, d)],
        out_specs=pl.BlockSpec((None, tm, d), lambda bi, i: (bi, i, 0)),
        out_shape=jax.ShapeDtypeStruct((b, t, d), F32),
        compiler_params=_cp("parallel", "parallel"),
        name="cross_prompt",
    )(x, g.reshape(1, d), wq, mk, mv, wo)


def _cross_sample_kernel(q_ref, mk_ref, mv_ref, o_ref):
    q = q_ref[0].astype(BF16).astype(F32)
    mk = mk_ref[0].astype(BF16).astype(F32)
    mv = mv_ref[0].astype(BF16).astype(F32)
    outs = []
    for h in range(H_X):
        sl = slice(h * HD_X, (h + 1) * HD_X)
        lg = jnp.sum(mk[:, sl] * q[:, sl], axis=1, keepdims=True) * (HD_X ** -0.5)
        pr = jnp.exp(lg - jnp.max(lg, axis=0, keepdims=True))
        pr = (pr / jnp.sum(pr, axis=0, keepdims=True)).astype(BF16).astype(F32)
        outs.append(jnp.sum(pr * mv[:, sl], axis=0, keepdims=True))
    o_ref[0] = jnp.concatenate(outs, axis=1)


def cross_sample(q, mk, mv):
    b = q.shape[0]
    mem = pl.BlockSpec((1, N_MEM, W_X), lambda i: (i, 0, 0))
    row = pl.BlockSpec((1, 1, W_X), lambda i: (i, 0, 0))
    return pl.pallas_call(
        _cross_sample_kernel,
        grid=(b,),
        in_specs=[row, mem, mem],
        out_specs=row,
        out_shape=jax.ShapeDtypeStruct((b, 1, W_X), F32),
        compiler_params=_cp("parallel"),
        name="cross_sample",
    )(q.reshape(b, 1, W_X), mk, mv).reshape(b, W_X)


def _group_weights(w_in):
    pad = lambda a, w: jnp.pad(a, ((0, 0), (0, 0), (0, w - a.shape[-1]))).astype(BF16)
    wa = pad(w_in[:, :, OFFS[0]:OFFS[6]], WG_A)
    wb = w_in[:, :, OFFS[6]:OFFS[10]].astype(BF16)
    wc = pad(w_in[:, :, OFFS[10]:OFFS[14]], WG_C)
    wg = w_in[:, :, OFFS[14]:OFFS[15]].astype(BF16)
    return wa, wb, wc, wg


def _rope_tables(pos):
    half = HD_A // 2
    inv = ROPE_THETA ** (-jnp.arange(half, dtype=F32) / half)
    ang = pos.astype(F32)[:, None] * inv[None, :]
    cos = jnp.tile(jnp.concatenate([jnp.cos(ang), jnp.cos(ang)], axis=1), (1, H_A))
    sin = jnp.tile(jnp.concatenate([-jnp.sin(ang), jnp.sin(ang)], axis=1), (1, H_A))
    return cos, sin


def _lower_bounds(lb_param):
    p = jax.nn.softmax(lb_param.astype(F32), axis=0)
    c = jnp.cumsum(p, axis=0)
    return c - c[0:1]


def kernel(x_prompt, x_sample, cache_attn_k, cache_attn_v, cache_idx_k, cache_mem_k, cache_mem_v, state_hgrn, state_gdn, state_conv, page_table, mem_prompt, norm_mix, w_in, g_kidx, lb_param, g_hgrn, conv_w, a_log, dt_bias, g_gdn, w_br_a, w_br_b, w_br_c, w_out, norm_cross, norm_mem, w_xq, w_xk, w_xv, w_xo, norm_mlp, w_up, w_down, norm_final):
    bp, tp, d = x_prompt.shape
    bs = x_sample.shape[0]
    past = page_table.shape[1] * PAGE
    npr = bp * tp
    tm_p = 512

    wa, wb, wc, wg = _group_weights(w_in)
    bf = lambda a: a.astype(BF16)
    w_br_a, w_br_b, w_br_c, w_out = bf(w_br_a), bf(w_br_b), bf(w_br_c), bf(w_out)
    w_xq, w_xk, w_xv, w_xo, w_up, w_down = bf(w_xq), bf(w_xk), bf(w_xv), bf(w_xo), bf(w_up), bf(w_down)
    lb = _lower_bounds(lb_param)
    gk = jnp.pad(g_kidx, ((0, 0), (0, LANES - D_IDX)))
    a_log_l = jnp.repeat(a_log, DK_C, axis=1)
    dt_bias_l = jnp.repeat(dt_bias, DK_C, axis=1)
    cos_p, sin_p = _rope_tables(jnp.arange(tp))
    cos_s, sin_s = _rope_tables(jnp.full((bs,), past))
    cache_kt = jnp.transpose(cache_attn_k, (0, 1, 3, 4, 2))
    cache_vt = jnp.transpose(cache_attn_v, (0, 1, 3, 4, 2))
    cache_kidx_t = jnp.transpose(cache_idx_k, (0, 1, 3, 2))
    mem_flat = mem_prompt.reshape(bp * N_MEM, d)
    zero_buf = jnp.zeros((bp, 8, CONV_DIM), F32)

    xp = x_prompt.reshape(npr, d)
    xs = x_sample.reshape(bs, d)
    outs = {k: [] for k in ("kp", "vp", "ip", "mkp", "mvp", "hp", "gp", "cp", "ks", "vs", "is", "hs", "gs", "cs")}
    for l in range(DEPTH):
        a_raw = norm_linear(xp, norm_mix[l], wa[l], tm_p)
        b_raw = norm_linear(xp, norm_mix[l], wb[l], tm_p)
        c_raw = norm_linear(xp, norm_mix[l], wc[l], tm_p)
        gates = norm_linear(xp, norm_mix[l], wg[l], tm_p)
        q, k, kb, qi, tail, ki2 = dsa_prep(a_raw, cos_p, sin_p, gk[l:l + 1], tm_p, tp // tm_p)
        vt = v_chunks_t(a_raw).reshape(bp, tp // SC, W_A, SC)
        r3 = lambda a: a.reshape(bp, tp, a.shape[-1])
        oa = dsa_prompt(r3(q), r3(qi), r3(tail), r3(ki2), r3(kb), vt).reshape(npr, W_A)
        ob, sh = hgrn_prompt(r3(b_raw), lb[l])
        kq, vc, eg, be = gdn_prep(r3(c_raw), zero_buf, conv_w[l], a_log_l[l:l + 1], dt_bias_l[l:l + 1], tm_p)
        oc, sg = gdn_prompt(kq, vc, eg, be)
        xp = merge(xp, gates, oa, ob.reshape(npr, W_B), oc.reshape(npr, W_C), b_raw, c_raw,
                   g_hgrn[l], g_gdn[l], w_br_a[l], w_br_b[l], w_br_c[l], w_out[l], tm_p)
        mk = norm_linear(mem_flat, norm_mem[l], w_xk[l], N_MEM)
        mv = norm_linear(mem_flat, norm_mem[l], w_xv[l], N_MEM)
        xp = cross_prompt(xp.reshape(bp, tp, d), norm_cross[l], w_xq[l], bf(mk).reshape(bp, N_MEM, W_X),
                          bf(mv).reshape(bp, N_MEM, W_X), w_xo[l], tm_p).reshape(npr, d)
        xp = mlp(xp, norm_mlp[l], w_up[l], w_down[l], 1024, 1024)
        outs["kp"].append(k.reshape(bp, tp, H_A, HD_A))
        outs["vp"].append(a_raw[:, 2 * W_A:3 * W_A].reshape(bp, tp, H_A, HD_A))
        outs["ip"].append(tail[:, 0:D_IDX].reshape(bp, tp, D_IDX))
        outs["mkp"].append(mk.reshape(bp, N_MEM, H_X, HD_X))
        outs["mvp"].append(mv.reshape(bp, N_MEM, H_X, HD_X))
        outs["hp"].append(sh)
        outs["gp"].append(sg)
        outs["cp"].append(r3(c_raw)[:, tp - (CONV_W - 1):, 0:CONV_DIM])

        a_raw = norm_linear(xs, norm_mix[l], wa[l], bs)
        b_raw = norm_linear(xs, norm_mix[l], wb[l], bs)
        c_raw = norm_linear(xs, norm_mix[l], wc[l], bs)
        gates = norm_linear(xs, norm_mix[l], wg[l], bs)
        q, k, kb, qi, tail, ki2 = dsa_prep(a_raw, cos_s, sin_s, gk[l:l + 1], bs, 1)
        va = a_raw[:, 2 * W_A:3 * W_A]
        sc = dsa_sample_scores(page_table, l, qi, tail, cache_kidx_t).reshape(bs, past)
        bias, new_sel = dsa_sample_select(sc, qi, tail)
        oa = dsa_sample_attend(page_table, l, q, k, va, bias, new_sel, cache_kt, cache_vt).reshape(bs, W_A)
        ob, sh = hgrn_sample(b_raw, lb[l], state_hgrn[l])
        oc, sg, conv_new = gdn_sample(c_raw, state_conv[l], conv_w[l], a_log[l], dt_bias[l], state_gdn[l])
        xs = merge(xs, gates, oa, ob, oc, b_raw, c_raw,
                   g_hgrn[l], g_gdn[l], w_br_a[l], w_br_b[l], w_br_c[l], w_out[l], bs)
        qx = norm_linear(xs, norm_cross[l], w_xq[l], bs)
        ox = cross_sample(qx, cache_mem_k[l].reshape(bs, N_MEM, W_X), cache_mem_v[l].reshape(bs, N_MEM, W_X))
        xs = linear_residual(xs, ox, w_xo[l], bs)
        xs = mlp(xs, norm_mlp[l], w_up[l], w_down[l], bs, 1024)
        outs["ks"].append(k.reshape(bs, 1, H_A, HD_A))
        outs["vs"].append(va.reshape(bs, 1, H_A, HD_A))
        outs["is"].append(tail[:, 0:D_IDX].reshape(bs, 1, D_IDX))
        outs["hs"].append(sh)
        outs["gs"].append(sg)
        outs["cs"].append(conv_new)

    y_prompt = final_norm(xp, norm_final, tm_p).reshape(bp, tp, d)
    y_sample = final_norm(xs, norm_final, bs).reshape(bs, 1, d)
    st = jnp.stack
    return (y_prompt, y_sample,
            st(outs["kp"]), st(outs["vp"]), st(outs["ip"]), st(outs["mkp"]), st(outs["mvp"]),
            st(outs["hp"]), st(outs["gp"]), st(outs["cp"]),
            st(outs["ks"]), st(outs["vs"]), st(outs["is"]), st(outs["hs"]), st(outs["gs"]), st(outs["cs"]))
```

```python
import functools
import math

import jax
import jax.numpy as jnp
import numpy as np
from jax import lax
from jax.experimental import pallas as pl
from jax.experimental.pallas import tpu as pltpu

F32 = jnp.float32
BF16 = jnp.bfloat16
I32 = jnp.int32

D_MODEL = 1024
DEPTH = 4
PAGE = 128
H_A, HD_A = 8, 64
H_IDX, D_IDX = 8, 64
TOPK = 256
ROPE_THETA = 10000.0
H_B, DK_B, DV_B = 4, 128, 128
H_C, DK_C, DV_C = 8, 64, 64
CONV_W = 4
N_MEM = 256
H_X, HD_X = 4, 128
D_FF = 4 * D_MODEL
EPS = 1e-6

W_A = H_A * HD_A
W_B = H_B * DK_B
W_C = H_C * DK_C
CONV_DIM = 3 * W_C
W_X = H_X * HD_X
SPLITS = (W_A, W_A, W_A, W_A, D_IDX, H_IDX, W_B, W_B, W_B, W_B, CONV_DIM, W_C, H_C, H_C, 3 * D_MODEL)
OFFS = tuple(int(o) for o in np.cumsum((0,) + SPLITS))
WG_A = 2176
WG_B = 2048
WG_C = 2176
WG_G = 3 * D_MODEL

LANES = 128
SUBLANES = 8
VMEM_LIMIT = 56 * 1024 * 1024
NEG_INF = float("-inf")
KEY_NEG_INF = -2139095041
HIGHEST = lax.Precision.HIGHEST

TQ = 128
SC = 512
SUB = 16
TB = 256


def _cp(*sem):
    return pltpu.CompilerParams(dimension_semantics=sem, vmem_limit_bytes=VMEM_LIMIT)


def _rms_rows(x, g):
    return x * lax.rsqrt(jnp.mean(x * x, axis=-1, keepdims=True) + EPS) * g


def _silu(x):
    return x * jax.nn.sigmoid(x)


def _dot(a, b):
    return jnp.dot(a, b, preferred_element_type=F32)


def _dot_nt(a, b):
    return lax.dot_general(a, b, (((1,), (1,)), ((), ())), preferred_element_type=F32)


def _dot_exact(a, b):
    return jnp.dot(a, b, preferred_element_type=F32, precision=HIGHEST)


def _norm_linear_kernel(x_ref, g_ref, w_ref, o_ref):
    h = _rms_rows(x_ref[...], g_ref[...])
    o_ref[...] = _dot(h.astype(BF16), w_ref[...])


def norm_linear(x, g, w, tm):
    n, d = x.shape
    wd = w.shape[1]
    return pl.pallas_call(
        _norm_linear_kernel,
        grid=(n // tm,),
        in_specs=[pl.BlockSpec((tm, d), lambda i: (i, 0)),
                  pl.BlockSpec((1, d), lambda i: (0, 0)),
                  pl.BlockSpec((d, wd), lambda i: (0, 0))],
        out_specs=pl.BlockSpec((tm, wd), lambda i: (i, 0)),
        out_shape=jax.ShapeDtypeStruct((n, wd), F32),
        compiler_params=_cp("parallel"),
        name="norm_linear",
    )(x, g.reshape(1, d), w)


def _linear_residual_kernel(x_ref, a_ref, w_ref, o_ref):
    o_ref[...] = x_ref[...] + _dot(a_ref[...].astype(BF16), w_ref[...])


def linear_residual(x, a, w, tm):
    n, d = x.shape
    k = a.shape[1]
    return pl.pallas_call(
        _linear_residual_kernel,
        grid=(n // tm,),
        in_specs=[pl.BlockSpec((tm, d), lambda i: (i, 0)),
                  pl.BlockSpec((tm, k), lambda i: (i, 0)),
                  pl.BlockSpec((k, d), lambda i: (0, 0))],
        out_specs=pl.BlockSpec((tm, d), lambda i: (i, 0)),
        out_shape=jax.ShapeDtypeStruct((n, d), F32),
        compiler_params=_cp("parallel"),
        name="linear_residual",
    )(x, a, w)


def _mlp_kernel(x_ref, g_ref, wu_ref, wd_ref, o_ref, h_scr):
    j = pl.program_id(1)

    @pl.when(j == 0)
    def _():
        x = x_ref[...]
        h_scr[...] = _rms_rows(x, g_ref[...]).astype(BF16)
        o_ref[...] = x

    u = _dot(h_scr[...], wu_ref[...])
    a = jnp.square(jnp.maximum(u, 0.0)).astype(BF16)
    o_ref[...] += _dot(a, wd_ref[...])


def mlp(x, g, w_up, w_down, tm, tf):
    n, d = x.shape
    ff = w_up.shape[1]
    return pl.pallas_call(
        _mlp_kernel,
        grid=(n // tm, ff // tf),
        in_specs=[pl.BlockSpec((tm, d), lambda i, j: (i, 0)),
                  pl.BlockSpec((1, d), lambda i, j: (0, 0)),
                  pl.BlockSpec((d, tf), lambda i, j: (0, j)),
                  pl.BlockSpec((tf, d), lambda i, j: (j, 0))],
        out_specs=pl.BlockSpec((tm, d), lambda i, j: (i, 0)),
        out_shape=jax.ShapeDtypeStruct((n, d), F32),
        scratch_shapes=[pltpu.VMEM((tm, d), BF16)],
        compiler_params=_cp("parallel", "arbitrary"),
        name="mlp",
    )(x, g.reshape(1, d), w_up, w_down)


def _final_norm_kernel(x_ref, g_ref, o_ref):
    o_ref[...] = _rms_rows(x_ref[...], g_ref[...])


def final_norm(x, g, tm):
    n, d = x.shape
    return pl.pallas_call(
        _final_norm_kernel,
        grid=(n // tm,),
        in_specs=[pl.BlockSpec((tm, d), lambda i: (i, 0)), pl.BlockSpec((1, d), lambda i: (0, 0))],
        out_specs=pl.BlockSpec((tm, d), lambda i: (i, 0)),
        out_shape=jax.ShapeDtypeStruct((n, d), F32),
        compiler_params=_cp("parallel"),
        name="final_norm",
    )(x, g.reshape(1, d))


def _rope(x, c, s):
    w = x.shape[-1]
    lane = lax.broadcasted_iota(I32, x.shape, 1)
    first = (lane % HD_A) < (HD_A // 2)
    partner = jnp.where(first, pltpu.roll(x, w - HD_A // 2, 1), pltpu.roll(x, HD_A // 2, 1))
    return x * c + partner * s


def _dsa_prep_kernel(a_ref, c_ref, s_ref, gk_ref, q_ref, k_ref, kb_ref, qi_ref, tail_ref, ki2_ref):
    c = c_ref[...]
    s = s_ref[...]
    qa = a_ref[:, 0:W_A]
    ka = a_ref[:, W_A:2 * W_A]
    qi = a_ref[:, 3 * W_A:4 * W_A]
    t = a_ref[:, 4 * W_A:4 * W_A + LANES]
    q_ref[...] = (_rope(qa, c, s) * (HD_A ** -0.5)).astype(BF16)
    kr = _rope(ka, c, s)
    k_ref[...] = kr
    kb_ref[...] = kr.astype(BF16)
    qi_ref[...] = _rope(qi, c, s).astype(BF16)
    lane = lax.broadcasted_iota(I32, t.shape, 1)
    is_k = lane < D_IDX
    ms = jnp.sum(jnp.where(is_k, t * t, 0.0), axis=-1, keepdims=True) * (1.0 / D_IDX)
    kin = t * lax.rsqrt(ms + EPS) * gk_ref[...]
    kir = _rope(kin, c[:, :LANES], s[:, :LANES])
    is_w = jnp.logical_and(lane >= D_IDX, lane < D_IDX + H_IDX)
    tail_ref[...] = kir + jnp.where(is_w, t * (H_IDX ** -0.5 * D_IDX ** -0.5), 0.0)
    ki2_ref[...] = (kir + pltpu.roll(kir, D_IDX, 1)).astype(BF16)


def dsa_prep(a_raw, cos, sin, gk, tm, n_tab_tiles):
    n = a_raw.shape[0]
    row = lambda w: pl.BlockSpec((tm, w), lambda i: (i, 0))
    tab = pl.BlockSpec((tm, W_A), lambda i: (i % n_tab_tiles, 0))
    sds = lambda w, dt: jax.ShapeDtypeStruct((n, w), dt)
    return pl.pallas_call(
        _dsa_prep_kernel,
        grid=(n // tm,),
        in_specs=[row(WG_A), tab, tab, pl.BlockSpec((1, LANES), lambda i: (0, 0))],
        out_specs=[row(W_A), row(W_A), row(W_A), row(W_A), row(LANES), row(LANES)],
        out_shape=[sds(W_A, BF16), sds(W_A, F32), sds(W_A, BF16), sds(W_A, BF16),
                   sds(LANES, F32), sds(LANES, BF16)],
        compiler_params=_cp("parallel"),
        name="dsa_prep",
    )(a_raw, cos, sin, gk)


def _v_chunks_kernel(a_ref, o_ref):
    o_ref[0] = a_ref[...].T.astype(BF16)


def v_chunks_t(a_raw):
    n = a_raw.shape[0]
    return pl.pallas_call(
        _v_chunks_kernel,
        grid=(n // SC,),
        in_specs=[pl.BlockSpec((SC, W_A), lambda i: (i, 2))],
        out_specs=pl.BlockSpec((1, W_A, SC), lambda i: (i, 0, 0)),
        out_shape=jax.ShapeDtypeStruct((n // SC, W_A, SC), BF16),
        compiler_params=_cp("parallel"),
        name="v_chunks_t",
    )(a_raw)


def _sort_key(x):
    i = pltpu.bitcast(x, I32)
    return i ^ (lax.shift_right_arithmetic(i, 31) & 0x7FFFFFFF)


def _kth_largest_key(count_ge, shape):
    def cond(s):
        return jnp.logical_and(s[0] < 32, s[3] == 0)

    def body(s):
        b, t, done, _ = s
        cand = t + lax.shift_left(jnp.int32(1), 31 - b)
        n = count_ge(cand)
        done = jnp.where(n == TOPK, 1, done)
        return b + 1, jnp.where(n >= TOPK, cand, t), done, jnp.min(done)

    init = (jnp.int32(0), jnp.full(shape, -2 ** 31, I32), jnp.zeros(shape, I32), jnp.int32(0))
    return lax.while_loop(cond, body, init)[1]


def _dsa_prompt_kernel(q_ref, qi_ref, tail_ref, ki2_ref, k_ref, vt_ref, o_ref,
                       qis_scr, qm_scr, key_scr, acc_scr):
    i = pl.program_id(1)
    t0 = i * TQ
    nchunk = (t0 + TQ + SC - 1) // SC
    lane = lax.broadcasted_iota(I32, (TQ, LANES), 1)
    half = lane // HD_A

    for h in range(H_A):
        p, hl = h // 2, h % 2
        sl = slice(p * LANES, (p + 1) * LANES)
        qis_scr[h * TQ:(h + 1) * TQ, :] = jnp.where(half == hl, qi_ref[:, sl].astype(F32), 0.0).astype(BF16)
        qm_scr[h * TQ:(h + 1) * TQ, :] = jnp.where(half == hl, q_ref[:, sl].astype(F32), 0.0).astype(BF16)
    w_t = tail_ref[...].T

    key_s = lax.broadcasted_iota(I32, (SC, TQ), 0)
    qry_t = t0 + lax.broadcasted_iota(I32, (SC, TQ), 1)

    def score_chunk(c, carry):
        base = pl.multiple_of(c * SC, SC)
        kc = ki2_ref[pl.ds(base, SC), :]
        acc = jnp.zeros((SC, TQ), F32)
        for h2 in range(H_IDX // 2):
            s2 = _dot_nt(kc, qis_scr[2 * h2 * TQ:(2 * h2 + 2) * TQ, :])
            for hl in range(2):
                h = 2 * h2 + hl
                acc = acc + w_t[D_IDX + h:D_IDX + h + 1, :] * jnp.maximum(s2[:, hl * TQ:(hl + 1) * TQ], 0.0)
        sc = jnp.where(key_s + base <= qry_t, acc, NEG_INF)
        key_scr[c] = _sort_key(sc)
        return carry

    lax.fori_loop(0, nchunk, score_chunk, 0)

    def count(pred):
        def body(c, part):
            hit = jnp.where(pred(key_scr[c], c), 1, 0)
            return part + jnp.sum(hit.reshape(SC // SUBLANES, SUBLANES, TQ), axis=0)
        part = lax.fori_loop(0, nchunk, body, jnp.zeros((SUBLANES, TQ), I32))
        return jnp.sum(part, axis=0, keepdims=True)

    thr = _kth_largest_key(lambda cand: count(lambda kx, c: kx >= cand), (1, TQ))
    n_ge = count(lambda kx, c: kx >= thr)
    excess = jnp.logical_and(n_ge > TOPK, thr > KEY_NEG_INF)

    @pl.when(jnp.max(excess.astype(I32)) > 0)
    def _():
        need = TOPK - count(lambda kx, c: kx > thr)

        def body(b, y):
            cand = y + lax.shift_left(jnp.int32(1), 13 - b)
            f = count(lambda kx, c: jnp.logical_and(kx == thr, key_s + c * SC < cand))
            return jnp.where(f < need, cand, y)

        cut = lax.fori_loop(0, 14, body, jnp.zeros((1, TQ), I32)) + 1
        cut = jnp.where(excess, cut, jnp.int32(2 ** 30))

        def drop(c, carry):
            kx = key_scr[c]
            dropped = jnp.logical_and(kx == thr, key_s + c * SC >= cut)
            key_scr[c] = jnp.where(dropped, KEY_NEG_INF, kx)
            return carry

        lax.fori_loop(0, nchunk, drop, 0)

    def to_bias(c, carry):
        sel = jnp.logical_and(key_scr[c] >= thr, key_s + c * SC <= qry_t)
        key_scr[c] = pltpu.bitcast(jnp.where(sel, 0.0, NEG_INF), I32)
        return carry

    lax.fori_loop(0, nchunk, to_bias, 0)

    npair = H_A // 2
    acc_scr[...] = jnp.zeros(acc_scr.shape, F32)

    def attend_chunk(c, carry):
        base = pl.multiple_of(c * SC, SC)
        bias = pltpu.bitcast(key_scr[c], F32)
        bias2 = jnp.concatenate([bias, bias], axis=1)
        pairs = [slice(p * LANES, (p + 1) * LANES) for p in range(npair)]
        lgs = [_dot_nt(k_ref[pl.ds(base, SC), pairs[p]], qm_scr[2 * p * TQ:(2 * p + 2) * TQ, :])
               for p in range(npair)]
        new, prs, alphas = [], [], []
        for p in range(npair):
            m_old, l_old = carry[p]
            x = lgs[p] + bias2
            m_new = jnp.maximum(m_old, jnp.max(x, axis=0, keepdims=True))
            m_safe = jnp.where(m_new == NEG_INF, 0.0, m_new)
            pr = jnp.exp(x - m_safe)
            alpha = jnp.exp(m_old - m_safe)
            new.append((m_new, alpha * l_old + jnp.sum(pr, axis=0, keepdims=True)))
            prs.append(pr.astype(BF16))
            alphas.append(alpha)
        pvs = [_dot(vt_ref[c, pairs[p], :], prs[p]) for p in range(npair)]
        for p in range(npair):
            acc_scr[p] = alphas[p] * acc_scr[p] + pvs[p]
        return tuple(new)

    init = tuple((jnp.full((1, 2 * TQ), NEG_INF, F32), jnp.zeros((1, 2 * TQ), F32)) for _ in range(npair))
    fin = lax.fori_loop(0, nchunk, attend_chunk, init)
    for p in range(npair):
        l_fin = fin[p][1]
        acc = acc_scr[p]
        o_t = jnp.concatenate([acc[0:HD_A, 0:TQ] / l_fin[:, 0:TQ],
                               acc[HD_A:2 * HD_A, TQ:2 * TQ] / l_fin[:, TQ:2 * TQ]], axis=0)
        o_ref[:, p * LANES:(p + 1) * LANES] = o_t.T


def dsa_prompt(q, qi, tail, ki2, kb, vt):
    b, t, _ = q.shape
    blk = lambda w: pl.BlockSpec((None, TQ, w), lambda bi, i: (bi, i, 0))
    full = lambda w: pl.BlockSpec((None, t, w), lambda bi, i: (bi, 0, 0), pipeline_mode=pl.Buffered(1))
    return pl.pallas_call(
        _dsa_prompt_kernel,
        grid=(b, t // TQ),
        in_specs=[blk(W_A), blk(W_A), blk(LANES), full(LANES), full(W_A),
                  pl.BlockSpec((None, t // SC, W_A, SC), lambda bi, i: (bi, 0, 0, 0),
                               pipeline_mode=pl.Buffered(1))],
        out_specs=blk(W_A),
        out_shape=jax.ShapeDtypeStruct((b, t, W_A), F32),
        scratch_shapes=[pltpu.VMEM((H_IDX * TQ, LANES), BF16),
                        pltpu.VMEM((H_A * TQ, LANES), BF16),
                        pltpu.VMEM((t // SC, SC, TQ), I32),
                        pltpu.VMEM((H_A // 2, LANES, 2 * TQ), F32)],
        compiler_params=_cp("parallel", "arbitrary"),
        name="dsa_prompt",
    )(q, qi, tail, ki2, kb, vt)


PG = 8


def _dsa_sample_score_kernel(pt_ref, qi_ref, w_ref, *refs):
    page_refs, o_ref = refs[:PG], refs[PG]
    j = pl.program_id(1)
    qh = qi_ref[0]
    w_col = w_ref[0]
    for g in range(PG):
        kpt = page_refs[g][...].astype(BF16)
        s = jnp.maximum(_dot(qh, kpt), 0.0)
        o_ref[0, :, pl.ds(pl.multiple_of((j * PG + g) * PAGE, PAGE), PAGE)] = jnp.sum(
            w_col * s, axis=0, keepdims=True)


def dsa_sample_scores(page_table, layer, qi, tail, pool_kidx_t):
    b, n_pages = page_table.shape
    page_spec = lambda g: pl.BlockSpec((None, None, D_IDX, PAGE),
                                       lambda bi, j, pt: (layer, pt[bi, j * PG + g], 0, 0))
    return pl.pallas_call(
        _dsa_sample_score_kernel,
        grid_spec=pltpu.PrefetchScalarGridSpec(
            num_scalar_prefetch=1,
            grid=(b, n_pages // PG),
            in_specs=[pl.BlockSpec((1, H_IDX, D_IDX), lambda bi, j, pt: (bi, 0, 0)),
                      pl.BlockSpec((1, H_IDX, 1), lambda bi, j, pt: (bi, 0, 0))]
                     + [page_spec(g) for g in range(PG)],
            out_specs=pl.BlockSpec((1, 1, n_pages * PAGE), lambda bi, j, pt: (bi, 0, 0)),
        ),
        out_shape=jax.ShapeDtypeStruct((b, 1, n_pages * PAGE), F32),
        compiler_params=_cp("parallel", "arbitrary"),
        name="dsa_sample_scores",
    )(page_table, qi.reshape(b, H_IDX, D_IDX), tail[:, D_IDX:D_IDX + H_IDX].reshape(b, H_IDX, 1),
      *([pool_kidx_t] * PG))


def _dsa_sample_select_kernel(sc_ref, qi_ref, tail_ref, bias_ref, newsel_ref):
    sc = sc_ref[...]
    nb, past = sc.shape
    qi = qi_ref[...]
    ki = tail_ref[...]
    lane = lax.broadcasted_iota(I32, ki.shape, 1)
    kidx = jnp.where(lane < D_IDX, ki, 0.0)
    kidx = (kidx + pltpu.roll(kidx, D_IDX, 1)).astype(BF16).astype(F32)
    sc_new = jnp.zeros((nb, 1), F32)
    for h in range(H_IDX):
        p, hl = h // 2, h % 2
        qh = qi[:, p * LANES:(p + 1) * LANES].astype(F32)
        d = jnp.sum(jnp.where(lane // D_IDX == hl, qh * kidx, 0.0), axis=1, keepdims=True)
        w = jnp.sum(jnp.where(lane == D_IDX + h, ki, 0.0), axis=1, keepdims=True)
        sc_new = sc_new + w * jnp.maximum(d, 0.0)
    key = _sort_key(sc + 0.0)
    key_new = _sort_key(sc_new + 0.0)

    def count_ge(cand):
        return (jnp.sum(jnp.where(key >= cand, 1, 0), axis=1, keepdims=True)
                + jnp.where(key_new >= cand, 1, 0))

    thr = _kth_largest_key(count_ge, (nb, 1))
    n_gt = jnp.sum(jnp.where(key > thr, 1, 0), axis=1, keepdims=True) + jnp.where(key_new > thr, 1, 0)
    need = TOPK - n_gt
    eq = key == thr
    pos = lax.broadcasted_iota(I32, sc.shape, 1)

    def body(b, y):
        cand = y + lax.shift_left(jnp.int32(1), 13 - b)
        f = jnp.sum(jnp.where(jnp.logical_and(eq, pos < cand), 1, 0), axis=1, keepdims=True)
        return jnp.where(f < need, cand, y)

    cut = lax.fori_loop(0, 14, body, jnp.zeros((nb, 1), I32)) + 1
    sel = jnp.logical_or(key > thr, jnp.logical_and(eq, pos < cut))
    n_sel = jnp.sum(jnp.where(sel, 1, 0), axis=1, keepdims=True)
    bias_ref[...] = jnp.where(sel, 0.0, NEG_INF)
    new_sel = jnp.logical_or(key_new > thr, jnp.logical_and(key_new == thr, n_sel < TOPK))
    newsel_ref[...] = jnp.broadcast_to(jnp.where(new_sel, 0.0, NEG_INF), newsel_ref.shape)


def dsa_sample_select(sc, qi, tail):
    b, past = sc.shape
    return pl.pallas_call(
        _dsa_sample_select_kernel,
        out_shape=[jax.ShapeDtypeStruct((b, past), F32), jax.ShapeDtypeStruct((b, LANES), F32)],
        compiler_params=pltpu.CompilerParams(vmem_limit_bytes=VMEM_LIMIT),
        name="dsa_sample_select",
    )(sc, qi, tail)


def _dsa_sample_attn_kernel(pt_ref, q_ref, kn_ref, vn_ref, bias_ref, nsel_ref, *refs):
    k_refs, v_refs, o_ref = refs[:PG], refs[PG:2 * PG], refs[2 * PG]
    m_scr, l_scr, acc_scr = refs[2 * PG + 1:]
    j = pl.program_id(1)
    qb = jnp.broadcast_to(q_ref[0], (H_A, HD_A, PAGE))
    lane0 = lax.broadcasted_iota(I32, (H_A, 1, PAGE), 2) == 0

    @pl.when(j == 0)
    def _():
        lg = jnp.sum(q_ref[0] * kn_ref[0], axis=1, keepdims=True) + nsel_ref[0][:, 0:1]
        m_scr[...] = lg
        pr = jnp.where(jnp.logical_and(lane0, lg > NEG_INF), 1.0, 0.0)
        l_scr[...] = pr
        acc_scr[...] = pr * vn_ref[0]

    for g in range(PG):
        bias = bias_ref[0, :, pl.ds(pl.multiple_of((j * PG + g) * PAGE, PAGE), PAGE)]
        x = jnp.sum(qb * k_refs[g][...], axis=1, keepdims=True) + bias
        m_old = m_scr[...]
        m_new = jnp.maximum(m_old, jnp.max(x, axis=2, keepdims=True))
        m_safe = jnp.where(m_new == NEG_INF, 0.0, m_new)
        pr = jnp.exp(x - m_safe)
        alpha = jnp.exp(m_old - m_safe)
        l_scr[...] = alpha * l_scr[...] + pr
        acc_scr[...] = alpha * acc_scr[...] + pr * v_refs[g][...]
        m_scr[...] = m_new

    @pl.when(j == pl.num_programs(1) - 1)
    def _():
        o_ref[0] = (jnp.sum(acc_scr[...], axis=2, keepdims=True)
                    / jnp.sum(l_scr[...], axis=2, keepdims=True))


def dsa_sample_attend(page_table, layer, q, k_new, v_new, bias, new_sel, cache_kt, cache_vt):
    b, n_pages = page_table.shape
    page_spec = lambda g: pl.BlockSpec((None, None, H_A, HD_A, PAGE),
                                       lambda bi, j, pt: (layer, pt[bi, j * PG + g], 0, 0, 0))
    cols = pl.BlockSpec((1, H_A, HD_A, 1), lambda bi, j, pt: (bi, 0, 0, 0))
    row = lambda w: pl.BlockSpec((1, 1, w), lambda bi, j, pt: (bi, 0, 0))
    col = lambda a: a.astype(F32).reshape(b, H_A, HD_A, 1)
    return pl.pallas_call(
        _dsa_sample_attn_kernel,
        grid_spec=pltpu.PrefetchScalarGridSpec(
            num_scalar_prefetch=1,
            grid=(b, n_pages // PG),
            in_specs=[cols, cols, cols, row(n_pages * PAGE), row(LANES)]
                     + [page_spec(g) for g in range(PG)] * 2,
            out_specs=cols,
            scratch_shapes=[pltpu.VMEM((H_A, 1, 1), F32), pltpu.VMEM((H_A, 1, PAGE), F32),
                            pltpu.VMEM((H_A, HD_A, PAGE), F32)],
        ),
        out_shape=jax.ShapeDtypeStruct((b, H_A, HD_A, 1), F32),
        compiler_params=_cp("parallel", "arbitrary"),
        name="dsa_sample_attend",
    )(page_table, col(q), col(k_new), col(v_new),
      bias.reshape(b, 1, n_pages * PAGE), new_sel.reshape(b, 1, LANES),
      *([cache_kt] * PG), *([cache_vt] * PG))


def _block_tri(n, blk, strict_upper):
    r = lax.broadcasted_iota(I32, (n, n), 0)
    c = lax.broadcasted_iota(I32, (n, n), 1)
    same = (r // blk) == (c // blk)
    tri = (c > r) if strict_upper else (c <= r)
    return jnp.where(jnp.logical_and(same, tri), 1.0, 0.0).astype(F32)


def _split_dot(x, w):
    hi = x.astype(BF16)
    lo = (x - hi.astype(F32)).astype(BF16)
    return _dot(hi, w) + _dot(lo, w)


def _hgrn_prompt_kernel(b_ref, lb_ref, llb_ref, o_ref, s_ref, st_scr):
    i = pl.program_id(1)
    tm = b_ref.shape[0]
    nsub = tm // SUB

    @pl.when(i == 0)
    def _():
        st_scr[...] = jnp.zeros(st_scr.shape, F32)

    lb = lb_ref[...]
    fb = b_ref[:, W_B:2 * W_B]
    q = _silu(b_ref[:, 0:W_B])
    v = b_ref[:, 2 * W_B:3 * W_B]
    kk = (1.0 - lb) * jax.nn.sigmoid(-fb)
    la = llb_ref[0:1, :]
    lbb = llb_ref[1:2, :] + (jnp.minimum(fb, 0.0) - jnp.log1p(jnp.exp(-jnp.abs(fb))))
    logf = jnp.maximum(la, lbb) + jnp.log1p(jnp.exp(-jnp.abs(la - lbb)))

    g = _dot_exact(_block_tri(tm, SUB, False), logf)
    r = _dot_exact(_block_tri(tm, SUB, True), logf)
    qg = (q * jnp.exp(g)).astype(BF16)
    kg = kk * jnp.exp(r)
    eg_last = jnp.exp(g + r)

    rows = lax.broadcasted_iota(I32, (tm, W_B), 0) % SUB
    hr = lax.broadcasted_iota(I32, (W_B, W_B), 0) // DK_B
    hc = lax.broadcasted_iota(I32, (W_B, W_B), 1) // DK_B
    head_ones = jnp.where(hr == hc, 1.0, 0.0).astype(BF16)
    o = jnp.zeros((tm, W_B), F32)
    for j in range(SUB):
        if j == 0:
            w = q * kk
            vj = v
        else:
            dec = jnp.exp(jnp.where(rows >= j, g - pltpu.roll(g, j, 0), NEG_INF))
            w = q * pltpu.roll(kk, j, 0) * dec
            vj = pltpu.roll(v, j, 0)
        o = o + _split_dot(w, head_ones) * vj

    rowi = lax.broadcasted_iota(I32, (tm, DK_B), 0) // SUB
    for h in range(H_B):
        sl = slice(h * DK_B, (h + 1) * DK_B)
        vt = v[:, sl].T.astype(BF16)
        qg_h = qg[:, sl]
        kg_h = kg[:, sl]
        st = st_scr[h]
        oh = jnp.zeros((tm, DV_B), F32)
        for c in range(nsub):
            inc = rowi == c
            oh = oh + jnp.where(inc, _dot_nt(qg_h, st.astype(BF16)), 0.0)
            st = st * eg_last[c * SUB:c * SUB + 1, sl] + _dot(vt, jnp.where(inc, kg_h, 0.0).astype(BF16))
        st_scr[h] = st
        o_ref[:, sl] = o[:, sl] + oh

    @pl.when(i == pl.num_programs(1) - 1)
    def _():
        for h in range(H_B):
            s_ref[h] = st_scr[h].T


def hgrn_prompt(b_raw, lb):
    b, t, _ = b_raw.shape
    tm = 128
    llb = jnp.stack([jnp.log(lb), jnp.log1p(-lb)])
    return pl.pallas_call(
        _hgrn_prompt_kernel,
        grid=(b, t // tm),
        in_specs=[pl.BlockSpec((None, tm, WG_B), lambda bi, i: (bi, i, 0)),
                  pl.BlockSpec((1, W_B), lambda bi, i: (0, 0)),
                  pl.BlockSpec((2, W_B), lambda bi, i: (0, 0))],
        out_specs=[pl.BlockSpec((None, tm, W_B), lambda bi, i: (bi, i, 0)),
                   pl.BlockSpec((None, H_B, DK_B, DV_B), lambda bi, i: (bi, 0, 0, 0))],
        out_shape=[jax.ShapeDtypeStruct((b, t, W_B), F32),
                   jax.ShapeDtypeStruct((b, H_B, DK_B, DV_B), F32)],
        scratch_shapes=[pltpu.VMEM((H_B, DV_B, DK_B), F32)],
        compiler_params=_cp("parallel", "arbitrary"),
        name="hgrn_prompt",
    )(b_raw, lb.reshape(1, W_B), llb)


def _hgrn_sample_kernel(qb_ref, fb_ref, ib_ref, lb_ref, s_ref, o_ref, sn_ref):
    lb = lb_ref[...]
    fb = fb_ref[0]
    f = lb + (1.0 - lb) * jax.nn.sigmoid(fb)
    kk = (1.0 - lb) * jax.nn.sigmoid(-fb)
    s_new = f * s_ref[0] + kk * ib_ref[0]
    sn_ref[0] = s_new
    o_ref[0] = jnp.sum(_silu(qb_ref[0]) * s_new, axis=1, keepdims=True)


def hgrn_sample(b_raw, lb, state):
    b = b_raw.shape[0]
    col = lambda a: a.reshape(b, H_B, DK_B, 1)
    cspec = pl.BlockSpec((1, H_B, DK_B, 1), lambda i: (i, 0, 0, 0))
    rspec = pl.BlockSpec((1, H_B, 1, DV_B), lambda i: (i, 0, 0, 0))
    sspec = pl.BlockSpec((1, H_B, DK_B, DV_B), lambda i: (i, 0, 0, 0))
    o, s_new = pl.pallas_call(
        _hgrn_sample_kernel,
        grid=(b,),
        in_specs=[cspec, cspec, rspec, pl.BlockSpec((H_B, DK_B, 1), lambda i: (0, 0, 0)), sspec],
        out_specs=[rspec, sspec],
        out_shape=[jax.ShapeDtypeStruct((b, H_B, 1, DV_B), F32),
                   jax.ShapeDtypeStruct((b, H_B, DK_B, DV_B), F32)],
        compiler_params=_cp("parallel"),
        name="hgrn_sample",
    )(col(b_raw[:, 0:W_B]), col(b_raw[:, W_B:2 * W_B]), b_raw[:, 2 * W_B:3 * W_B].reshape(b, H_B, 1, DV_B),
      lb.reshape(H_B, DK_B, 1), state)
    return o.reshape(b, W_B), s_new


def _head_sum_matrix():
    r = lax.broadcasted_iota(I32, (W_C, W_C), 0) // DK_C
    c = lax.broadcasted_iota(I32, (W_C, W_C), 1) // DK_C
    return jnp.where(r == c, 1.0, 0.0).astype(F32)


def _gdn_prep_kernel(c_ref, halo_ref, buf_ref, w_ref, al_ref, dt_ref, kq_ref, v_ref, eg_ref, be_ref):
    i = pl.program_id(1)
    tm = c_ref.shape[0]
    prev = jnp.where(i == 0, buf_ref[...], halo_ref[:, 0:CONV_DIM])
    win = jnp.concatenate([prev, c_ref[:, 0:CONV_DIM]], axis=0)
    y = jnp.zeros((tm, CONV_DIM), F32)
    for tap in range(CONV_W):
        off = 8 - (CONV_W - 1) + tap
        y = y + win[off:off + tm, :] * w_ref[tap:tap + 1, :]
    y = _silu(y)
    ones = _head_sum_matrix().astype(BF16)
    qc = y[:, 0:W_C]
    kc = y[:, W_C:2 * W_C]
    qn = qc * lax.rsqrt(_split_dot(qc * qc, ones) + EPS) * (DK_C ** -0.5)
    kn = kc * lax.rsqrt(_split_dot(kc * kc, ones) + EPS)
    kbits = pltpu.bitcast(kn.astype(BF16).astype(F32), I32) & jnp.int32(-65536)
    qbits = lax.shift_right_logical(pltpu.bitcast(qn.astype(BF16).astype(F32), I32), 16)
    kq_ref[...] = kbits | qbits
    v_ref[...] = y[:, 2 * W_C:3 * W_C]
    tail = c_ref[:, CONV_DIM + W_C:CONV_DIM + W_C + LANES]
    r = lax.broadcasted_iota(I32, (LANES, W_C), 0)
    cc = lax.broadcasted_iota(I32, (LANES, W_C), 1) // DK_C
    bcl = _dot_exact(tail, jnp.where(r == cc, 1.0, 0.0).astype(F32))
    acl = _dot_exact(tail, jnp.where(r == cc + H_C, 1.0, 0.0).astype(F32))
    be_ref[...] = jax.nn.sigmoid(bcl)
    x = acl + dt_ref[...]
    softplus = jnp.maximum(x, 0.0) + jnp.log1p(jnp.exp(-jnp.abs(x)))
    eg_ref[...] = jnp.exp(-jnp.exp(al_ref[...]) * softplus)


def gdn_prep(c_raw, buf8, conv_w, a_log_l, dt_bias_l, tm):
    b, t, _ = c_raw.shape
    blk = lambda w: pl.BlockSpec((None, tm, w), lambda bi, i: (bi, i, 0))
    sds = jax.ShapeDtypeStruct((b, t, W_C), F32)
    return pl.pallas_call(
        _gdn_prep_kernel,
        grid=(b, t // tm),
        in_specs=[blk(WG_C),
                  pl.BlockSpec((None, 8, WG_C), lambda bi, i: (bi, jnp.maximum(i * (tm // 8) - 1, 0), 0)),
                  pl.BlockSpec((None, 8, CONV_DIM), lambda bi, i: (bi, 0, 0)),
                  pl.BlockSpec((CONV_W, CONV_DIM), lambda bi, i: (0, 0)),
                  pl.BlockSpec((1, W_C), lambda bi, i: (0, 0)),
                  pl.BlockSpec((1, W_C), lambda bi, i: (0, 0))],
        out_specs=[blk(W_C)] * 4,
        out_shape=[jax.ShapeDtypeStruct((b, t, W_C), I32), sds, sds, sds],
        compiler_params=_cp("parallel", "arbitrary"),
        name="gdn_prep",
    )(c_raw, c_raw, buf8, conv_w, a_log_l, dt_bias_l)


def _gdn_prompt_kernel(kq_ref, v_ref, eg_ref, be_ref, o_ref, s_ref, z_scr):
    i = pl.program_id(1)

    @pl.when(i == 0)
    def _():
        z_scr[...] = jnp.zeros(z_scr.shape, F32)

    ri = lax.broadcasted_iota(I32, (DK_C, LANES), 0)
    lj = lax.broadcasted_iota(I32, (DK_C, LANES), 1) % DV_C
    keep = lj >= ri

    def rot(r):
        a = pltpu.roll(jnp.broadcast_to(r, (DK_C, LANES)), 0, 1, stride=1, stride_axis=0)
        return jnp.where(keep, a, pltpu.roll(a, DV_C, 1))

    def step(t, carry):
        kqrow = kq_ref[pl.ds(t, 1), :]
        vrow = v_ref[pl.ds(t, 1), :]
        egrow = eg_ref[pl.ds(t, 1), :]
        berow = be_ref[pl.ds(t, 1), :]
        outs = []
        for p in range(H_C // 2):
            sl = slice(p * LANES, (p + 1) * LANES)
            word = rot(kqrow[:, sl])
            kk = pltpu.bitcast(word & jnp.int32(-65536), F32)
            qq = pltpu.bitcast(lax.shift_left(word, 16), F32)
            z = z_scr[p]
            ks = jnp.sum(kk * z, axis=0, keepdims=True)
            u = berow[:, sl] * (vrow[:, sl] - egrow[:, sl] * ks)
            z = egrow[:, sl] * z + kk * u
            z_scr[p] = z
            outs.append(jnp.sum(qq * z, axis=0, keepdims=True))
        o_ref[pl.ds(t, 1), :] = jnp.concatenate(outs, axis=1)
        return carry

    lax.fori_loop(0, kq_ref.shape[0], step, 0, unroll=2)

    @pl.when(i == pl.num_programs(1) - 1)
    def _():
        s_ref[...] = z_scr[...]


def _unrotate(z):
    n = z.shape[-1]
    d = jnp.arange(n)[:, None]
    v = jnp.arange(n)[None, :]
    idx = jnp.broadcast_to((v - d) % n, z.shape)
    return jnp.take_along_axis(z, idx, axis=-2)


def gdn_prompt(kq, v, eg, be):
    b, t, _ = v.shape
    blk = pl.BlockSpec((None, TB, W_C), lambda bi, i: (bi, i, 0))
    o, z = pl.pallas_call(
        _gdn_prompt_kernel,
        grid=(b, t // TB),
        in_specs=[blk] * 4,
        out_specs=[blk, pl.BlockSpec((None, H_C // 2, DK_C, LANES), lambda bi, i: (bi, 0, 0, 0))],
        out_shape=[jax.ShapeDtypeStruct((b, t, W_C), F32),
                   jax.ShapeDtypeStruct((b, H_C // 2, DK_C, LANES), F32)],
        scratch_shapes=[pltpu.VMEM((H_C // 2, DK_C, LANES), F32)],
        compiler_params=_cp("parallel", "arbitrary"),
        name="gdn_prompt",
    )(kq, v, eg, be)
    z = z.reshape(b, H_C // 2, DK_C, 2, DV_C).transpose(0, 1, 3, 2, 4).reshape(b, H_C, DK_C, DV_C)
    return o, _unrotate(z)


def _gdn_sample_kernel(qk_ref, vv_ref, wqk_ref, wv_ref, bc_ref, ac_ref, al_ref, dt_ref, s_ref, o_ref, sn_ref):
    yqk = jnp.zeros(qk_ref.shape[2:], F32)
    yv = jnp.zeros(vv_ref.shape[2:], F32)
    for tap in range(CONV_W):
        yqk = yqk + qk_ref[0, tap] * wqk_ref[tap]
        yv = yv + vv_ref[0, tap] * wv_ref[tap]
    yqk = _silu(yqk)
    vrow = _silu(yv)
    yq, yk = yqk[0:H_C], yqk[H_C:2 * H_C]
    qcol = yq * lax.rsqrt(jnp.sum(yq * yq, axis=1, keepdims=True) + EPS) * (DK_C ** -0.5)
    kcol = yk * lax.rsqrt(jnp.sum(yk * yk, axis=1, keepdims=True) + EPS)
    beta = jax.nn.sigmoid(bc_ref[0])
    x = ac_ref[0] + dt_ref[...]
    eg = jnp.exp(-jnp.exp(al_ref[...]) * (jnp.maximum(x, 0.0) + jnp.log1p(jnp.exp(-jnp.abs(x)))))
    s = s_ref[0]
    ks = jnp.sum(kcol * s, axis=1, keepdims=True)
    u = beta * (vrow - eg * ks)
    s_new = eg * s + kcol * u
    sn_ref[0] = s_new
    o_ref[0] = jnp.sum(qcol * s_new, axis=1, keepdims=True)


def gdn_sample(c_raw, conv_buf, conv_w, a_log_l, dt_bias_l, state):
    b = c_raw.shape[0]
    taps = jnp.concatenate([conv_buf, c_raw[:, None, 0:CONV_DIM]], axis=1)
    qk = taps[:, :, 0:2 * W_C].reshape(b, CONV_W, 2 * H_C, DK_C, 1)
    vv = taps[:, :, 2 * W_C:].reshape(b, CONV_W, H_C, 1, DV_C)
    wqk = conv_w[:, 0:2 * W_C].reshape(CONV_W, 2 * H_C, DK_C, 1)
    wv = conv_w[:, 2 * W_C:].reshape(CONV_W, H_C, 1, DV_C)
    bc = c_raw[:, CONV_DIM + W_C:CONV_DIM + W_C + H_C].reshape(b, H_C, 1, 1)
    ac = c_raw[:, CONV_DIM + W_C + H_C:CONV_DIM + W_C + 2 * H_C].reshape(b, H_C, 1, 1)
    per_seq = lambda shp: pl.BlockSpec((1,) + shp, lambda i: (i,) + (0,) * len(shp))
    const = lambda shp: pl.BlockSpec(shp, lambda i: (0,) * len(shp))
    o, s_new = pl.pallas_call(
        _gdn_sample_kernel,
        grid=(b,),
        in_specs=[per_seq((CONV_W, 2 * H_C, DK_C, 1)), per_seq((CONV_W, H_C, 1, DV_C)),
                  const((CONV_W, 2 * H_C, DK_C, 1)), const((CONV_W, H_C, 1, DV_C)),
                  per_seq((H_C, 1, 1)), per_seq((H_C, 1, 1)), const((H_C, 1, 1)), const((H_C, 1, 1)),
                  per_seq((H_C, DK_C, DV_C))],
        out_specs=[per_seq((H_C, 1, DV_C)), per_seq((H_C, DK_C, DV_C))],
        out_shape=[jax.ShapeDtypeStruct((b, H_C, 1, DV_C), F32),
                   jax.ShapeDtypeStruct((b, H_C, DK_C, DV_C), F32)],
        compiler_params=_cp("parallel"),
        name="gdn_sample",
    )(qk, vv, wqk, wv, bc, ac, a_log_l.reshape(H_C, 1, 1), dt_bias_l.reshape(H_C, 1, 1), state)
    return o.reshape(b, W_C), s_new, taps[:, 1:, :]


def _merge_kernel(x_ref, g_ref, oa_ref, ob_ref, oc_ref, gb_ref, zc_ref, gh_ref, gg_ref,
                  wa_ref, wb_ref, wc_ref, wo_ref, o_ref):
    ob = ob_ref[...]
    parts = []
    for h in range(H_B):
        seg = ob[:, h * DV_B:(h + 1) * DV_B]
        parts.append(seg * lax.rsqrt(jnp.mean(seg * seg, axis=-1, keepdims=True) + EPS))
    obn = jnp.concatenate(parts, axis=1) * gh_ref[...] * _silu(gb_ref[...])
    oc = oc_ref[...]
    ms = _split_dot(oc * oc, _head_sum_matrix().astype(BF16)) * (1.0 / DV_C)
    ocn = oc * lax.rsqrt(ms + EPS) * gg_ref[...] * _silu(zc_ref[...])
    m = (jax.nn.sigmoid(g_ref[:, 0:D_MODEL]) * _dot(oa_ref[...].astype(BF16), wa_ref[...])
         + jax.nn.sigmoid(g_ref[:, D_MODEL:2 * D_MODEL]) * _dot(obn.astype(BF16), wb_ref[...])
         + jax.nn.sigmoid(g_ref[:, 2 * D_MODEL:3 * D_MODEL]) * _dot(ocn.astype(BF16), wc_ref[...]))
    o_ref[...] = x_ref[...] + _dot(m.astype(BF16), wo_ref[...])


def merge(x, gates, oa, ob, oc, b_raw, c_raw, g_hgrn, g_gdn, wa, wb, wc, wo, tm):
    n = x.shape[0]
    row = lambda w: pl.BlockSpec((tm, w), lambda i: (i, 0))
    const = lambda r, c: pl.BlockSpec((r, c), lambda i: (0, 0))
    return pl.pallas_call(
        _merge_kernel,
        grid=(n // tm,),
        in_specs=[row(D_MODEL), row(WG_G), row(W_A), row(W_B), row(W_C),
                  pl.BlockSpec((tm, W_B), lambda i: (i, 3)),
                  pl.BlockSpec((tm, W_C), lambda i: (i, 3)),
                  const(1, W_B), const(1, W_C),
                  const(W_A, D_MODEL), const(W_B, D_MODEL), const(W_C, D_MODEL), const(D_MODEL, D_MODEL)],
        out_specs=row(D_MODEL),
        out_shape=jax.ShapeDtypeStruct((n, D_MODEL), F32),
        compiler_params=_cp("parallel"),
        name="merge",
    )(x, gates, oa, ob, oc, b_raw, c_raw, g_hgrn.reshape(1, W_B), g_gdn.reshape(1, W_C), wa, wb, wc, wo)


def _cross_prompt_kernel(x_ref, g_ref, wq_ref, mk_ref, mv_ref, wo_ref, o_ref):
    x = x_ref[...]
    q = _dot(_rms_rows(x, g_ref[...]).astype(BF16), wq_ref[...])
    outs = []
    for h in range(H_X):
        sl = slice(h * HD_X, (h + 1) * HD_X)
        lg = _dot_nt(q[:, sl].astype(BF16), mk_ref[:, sl]) * (HD_X ** -0.5)
        pr = jnp.exp(lg - jnp.max(lg, axis=-1, keepdims=True))
        pr = pr / jnp.sum(pr, axis=-1, keepdims=True)
        outs.append(_dot(pr.astype(BF16), mv_ref[:, sl]))
    o_ref[...] = x + _dot(jnp.concatenate(outs, axis=1).astype(BF16), wo_ref[...])


def cross_prompt(x, g, wq, mk, mv, wo, tm):
    b, t, d = x.shape
    const = lambda r, c: pl.BlockSpec((r, c), lambda bi, i: (0, 0))
    mem = pl.BlockSpec((None, N_MEM, W_X), lambda bi, i: (bi, 0, 0))
    return pl.pallas_call(
        _cross_prompt_kernel,
        grid=(b, t // tm),
        in_specs=[pl.BlockSpec((None, tm, d), lambda bi, i: (bi, i, 0)), const(1, d), const(d, W_X),
                  mem, mem, const(W_X, d)],
        out_specs=pl.BlockSpec((None, tm, d), lambda bi, i: (bi, i, 0)),
        out_shape=jax.ShapeDtypeStruct((b, t, d), F32),
        compiler_params=_cp("parallel", "parallel"),
        name="cross_prompt",
    )(x, g.reshape(1, d), wq, mk, mv, wo)


def _cross_sample_kernel(q_ref, mk_ref, mv_ref, o_ref):
    q = q_ref[0].astype(BF16).astype(F32)
    mk = mk_ref[0].astype(BF16).astype(F32)
    mv = mv_ref[0].astype(BF16).astype(F32)
    outs = []
    for h in range(H_X):
        sl = slice(h * HD_X, (h + 1) * HD_X)
        lg = jnp.sum(mk[:, sl] * q[:, sl], axis=1, keepdims=True) * (HD_X ** -0.5)
        pr = jnp.exp(lg - jnp.max(lg, axis=0, keepdims=True))
        pr = (pr / jnp.sum(pr, axis=0, keepdims=True)).astype(BF16).astype(F32)
        outs.append(jnp.sum(pr * mv[:, sl], axis=0, keepdims=True))
    o_ref[0] = jnp.concatenate(outs, axis=1)


def cross_sample(q, mk, mv):
    b = q.shape[0]
    mem = pl.BlockSpec((1, N_MEM, W_X), lambda i: (i, 0, 0))
    row = pl.BlockSpec((1, 1, W_X), lambda i: (i, 0, 0))
    return pl.pallas_call(
        _cross_sample_kernel,
        grid=(b,),
        in_specs=[row, mem, mem],
        out_specs=row,
        out_shape=jax.ShapeDtypeStruct((b, 1, W_X), F32),
        compiler_params=_cp("parallel"),
        name="cross_sample",
    )(q.reshape(b, 1, W_X), mk, mv).reshape(b, W_X)


def _group_weights(w_in):
    pad = lambda a, w: jnp.pad(a, ((0, 0), (0, 0), (0, w - a.shape[-1]))).astype(BF16)
    wa = pad(w_in[:, :, OFFS[0]:OFFS[6]], WG_A)
    wb = w_in[:, :, OFFS[6]:OFFS[10]].astype(BF16)
    wc = pad(w_in[:, :, OFFS[10]:OFFS[14]], WG_C)
    wg = w_in[:, :, OFFS[14]:OFFS[15]].astype(BF16)
    return wa, wb, wc, wg


def _rope_tables(pos):
    half = HD_A // 2
    inv = ROPE_THETA ** (-jnp.arange(half, dtype=F32) / half)
    ang = pos.astype(F32)[:, None] * inv[None, :]
    cos = jnp.tile(jnp.concatenate([jnp.cos(ang), jnp.cos(ang)], axis=1), (1, H_A))
    sin = jnp.tile(jnp.concatenate([-jnp.sin(ang), jnp.sin(ang)], axis=1), (1, H_A))
    return cos, sin


def _lower_bounds(lb_param):
    p = jax.nn.softmax(lb_param.astype(F32), axis=0)
    c = jnp.cumsum(p, axis=0)
    return c - c[0:1]


def kernel(x_prompt, x_sample, cache_attn_k, cache_attn_v, cache_idx_k, cache_mem_k, cache_mem_v, state_hgrn, state_gdn, state_conv, page_table, mem_prompt, norm_mix, w_in, g_kidx, lb_param, g_hgrn, conv_w, a_log, dt_bias, g_gdn, w_br_a, w_br_b, w_br_c, w_out, norm_cross, norm_mem, w_xq, w_xk, w_xv, w_xo, norm_mlp, w_up, w_down, norm_final):
    bp, tp, d = x_prompt.shape
    bs = x_sample.shape[0]
    past = page_table.shape[1] * PAGE
    npr = bp * tp
    tm_p = 512

    wa, wb, wc, wg = _group_weights(w_in)
    bf = lambda a: a.astype(BF16)
    w_br_a, w_br_b, w_br_c, w_out = bf(w_br_a), bf(w_br_b), bf(w_br_c), bf(w_out)
    w_xq, w_xk, w_xv, w_xo, w_up, w_down = bf(w_xq), bf(w_xk), bf(w_xv), bf(w_xo), bf(w_up), bf(w_down)
    lb = _lower_bounds(lb_param)
    gk = jnp.pad(g_kidx, ((0, 0), (0, LANES - D_IDX)))
    a_log_l = jnp.repeat(a_log, DK_C, axis=1)
    dt_bias_l = jnp.repeat(dt_bias, DK_C, axis=1)
    cos_p, sin_p = _rope_tables(jnp.arange(tp))
    cos_s, sin_s = _rope_tables(jnp.full((bs,), past))
    cache_kt = jnp.transpose(cache_attn_k, (0, 1, 3, 4, 2))
    cache_vt = jnp.transpose(cache_attn_v, (0, 1, 3, 4, 2))
    cache_kidx_t = jnp.transpose(cache_idx_k, (0, 1, 3, 2))
    mem_flat = mem_prompt.reshape(bp * N_MEM, d)
    zero_buf = jnp.zeros((bp, 8, CONV_DIM), F32)

    xp = x_prompt.reshape(npr, d)
    xs = x_sample.reshape(bs, d)
    outs = {k: [] for k in ("kp", "vp", "ip", "mkp", "mvp", "hp", "gp", "cp", "ks", "vs", "is", "hs", "gs", "cs")}
    for l in range(DEPTH):
        a_raw = norm_linear(xp, norm_mix[l], wa[l], tm_p)
        b_raw = norm_linear(xp, norm_mix[l], wb[l], tm_p)
        c_raw = norm_linear(xp, norm_mix[l], wc[l], tm_p)
        gates = norm_linear(xp, norm_mix[l], wg[l], tm_p)
        q, k, kb, qi, tail, ki2 = dsa_prep(a_raw, cos_p, sin_p, gk[l:l + 1], tm_p, tp // tm_p)
        vt = v_chunks_t(a_raw).reshape(bp, tp // SC, W_A, SC)
        r3 = lambda a: a.reshape(bp, tp, a.shape[-1])
        oa = dsa_prompt(r3(q), r3(qi), r3(tail), r3(ki2), r3(kb), vt).reshape(npr, W_A)
        ob, sh = hgrn_prompt(r3(b_raw), lb[l])
        kq, vc, eg, be = gdn_prep(r3(c_raw), zero_buf, conv_w[l], a_log_l[l:l + 1], dt_bias_l[l:l + 1], tm_p)
        oc, sg = gdn_prompt(kq, vc, eg, be)
        xp = merge(xp, gates, oa, ob.reshape(npr, W_B), oc.reshape(npr, W_C), b_raw, c_raw,
                   g_hgrn[l], g_gdn[l], w_br_a[l], w_br_b[l], w_br_c[l], w_out[l], tm_p)
        mk = norm_linear(mem_flat, norm_mem[l], w_xk[l], N_MEM)
        mv = norm_linear(mem_flat, norm_mem[l], w_xv[l], N_MEM)
        xp = cross_prompt(xp.reshape(bp, tp, d), norm_cross[l], w_xq[l], bf(mk).reshape(bp, N_MEM, W_X),
                          bf(mv).reshape(bp, N_MEM, W_X), w_xo[l], tm_p).reshape(npr, d)
        xp = mlp(xp, norm_mlp[l], w_up[l], w_down[l], 1024, 1024)
        outs["kp"].append(k.reshape(bp, tp, H_A, HD_A))
        outs["vp"].append(a_raw[:, 2 * W_A:3 * W_A].reshape(bp, tp, H_A, HD_A))
        outs["ip"].append(tail[:, 0:D_IDX].reshape(bp, tp, D_IDX))
        outs["mkp"].append(mk.reshape(bp, N_MEM, H_X, HD_X))
        outs["mvp"].append(mv.reshape(bp, N_MEM, H_X, HD_X))
        outs["hp"].append(sh)
        outs["gp"].append(sg)
        outs["cp"].append(r3(c_raw)[:, tp - (CONV_W - 1):, 0:CONV_DIM])

        a_raw = norm_linear(xs, norm_mix[l], wa[l], bs)
        b_raw = norm_linear(xs, norm_mix[l], wb[l], bs)
        c_raw = norm_linear(xs, norm_mix[l], wc[l], bs)
        gates = norm_linear(xs, norm_mix[l], wg[l], bs)
        q, k, kb, qi, tail, ki2 = dsa_prep(a_raw, cos_s, sin_s, gk[l:l + 1], bs, 1)
        va = a_raw[:, 2 * W_A:3 * W_A]
        sc = dsa_sample_scores(page_table, l, qi, tail, cache_kidx_t).reshape(bs, past)
        bias, new_sel = dsa_sample_select(sc, qi, tail)
        oa = dsa_sample_attend(page_table, l, q, k, va, bias, new_sel, cache_kt, cache_vt).reshape(bs, W_A)
        ob, sh = hgrn_sample(b_raw, lb[l], state_hgrn[l])
        oc, sg, conv_new = gdn_sample(c_raw, state_conv[l], conv_w[l], a_log[l], dt_bias[l], state_gdn[l])
        xs = merge(xs, gates, oa, ob, oc, b_raw, c_raw,
                   g_hgrn[l], g_gdn[l], w_br_a[l], w_br_b[l], w_br_c[l], w_out[l], bs)
        qx = norm_linear(xs, norm_cross[l], w_xq[l], bs)
        ox = cross_sample(qx, cache_mem_k[l].reshape(bs, N_MEM, W_X), cache_mem_v[l].reshape(bs, N_MEM, W_X))
        xs = linear_residual(xs, ox, w_xo[l], bs)
        xs = mlp(xs, norm_mlp[l], w_up[l], w_down[l], bs, 1024)
        outs["ks"].append(k.reshape(bs, 1, H_A, HD_A))
        outs["vs"].append(va.reshape(bs, 1, H_A, HD_A))
        outs["is"].append(tail[:, 0:D_IDX].reshape(bs, 1, D_IDX))
        outs["hs"].append(sh)
        outs["gs"].append(sg)
        outs["cs"].append(conv_new)

    y_prompt = final_norm(xp, norm_final, tm_p).reshape(bp, tp, d)
    y_sample = final_norm(xs, norm_final, bs).reshape(bs, 1, d)
    st = jnp.stack
    return (y_prompt, y_sample,
            st(outs["kp"]), st(outs["vp"]), st(outs["ip"]), st(outs["mkp"]), st(outs["mvp"]),
            st(outs["hp"]), st(outs["gp"]), st(outs["cp"]),
            st(outs["ks"]), st(outs["vs"]), st(outs["is"]), st(outs["hs"]), st(outs["gs"]), st(outs["cs"]))
```
